```python
import math
import jax, jax.numpy as jnp
from jax import lax
import numpy as np

D_MODEL = 1024
BATCH = 2
SEQ = 8192
DEPTH = 4
DEC_BATCH = 8
DEC_SEQ = 64
PAST_LEN = 2048

CHUNK = 64
N_EVEN = (DEPTH + 1) // 2
N_ODD = DEPTH // 2
ROPE_THETA = 10000.0
EPS = 1e-5
A_HEADS = 4
A_DH = 64
A_WIDTH = A_HEADS * 2 * A_DH
Q_BLOCK = 128
POOL_WINDOWS = (2, 4, 8, 16)
POOL_GROUPS = len(POOL_WINDOWS)
B_WIDTH = D_MODEL - A_WIDTH
POOL_CH = B_WIDTH // POOL_GROUPS
POOL_HIST = max(POOL_WINDOWS) - 1
EVEN_IN = 3 * A_WIDTH + B_WIDTH
C_HEADS = 16
C_KV = 2
C_DH = D_MODEL // C_HEADS
C_GROUP = C_HEADS // C_KV
WINDOW = 128
WIN_CHUNKS = WINDOW // CHUNK
C_CACHE = min(WINDOW, PAST_LEN)
ODD_IN = C_HEADS * C_DH + 2 * C_KV * C_DH
D_FF = 2816
CONV_W = 3

kernel_name = 'hybrid_stream_encoder_step'


def rmsnorm(x, g):
    xf = x.astype(jnp.float32)
    y = xf * lax.rsqrt(jnp.mean(xf * xf, axis=-1, keepdims=True) + EPS)
    return (y * g.astype(jnp.float32)).astype(x.dtype)


def rope(x, pos):
    d = x.shape[-1]
    inv = ROPE_THETA ** (-jnp.arange(0, d, 2, dtype=jnp.float32) / d)
    ang = pos.astype(jnp.float32)[:, None] * inv[None, :]
    shape = (ang.shape[0],) + (1,) * (x.ndim - 3) + (d // 2,)
    cos = jnp.cos(ang).reshape(shape)
    sin = jnp.sin(ang).reshape(shape)
    xf = x.astype(jnp.float32)
    x1, x2 = xf[..., :d // 2], xf[..., d // 2:]
    return jnp.concatenate([x1 * cos - x2 * sin, x2 * cos + x1 * sin], axis=-1).astype(x.dtype)


def diff_core(q, k, v, lam, mask):
    s = jnp.einsum('bqhmd,bkhmd->bhmqk', q, k).astype(jnp.float32) * (A_DH ** -0.5)
    if mask is not None:
        s = jnp.where(mask, s, -jnp.inf)
    p = jax.nn.softmax(s, axis=-1)
    a = p[:, :, 0] - lam * p[:, :, 1]
    return jnp.einsum('bhqk,bkhe->bqhe', a.astype(v.dtype), v)


def diff_attn_prompt(q, k, v, lam):
    B, S = q.shape[:2]
    nb = S // Q_BLOCK
    qb = q.reshape(B, nb, Q_BLOCK, A_HEADS, 2, A_DH).swapaxes(0, 1)
    k_chunk = jnp.arange(S) // CHUNK

    def one(args):
        q_blk, bi = args
        q_chunk = (bi * Q_BLOCK + jnp.arange(Q_BLOCK)) // CHUNK
        mask = k_chunk[None, :] <= q_chunk[:, None]
        return diff_core(q_blk, k, v, lam, mask)

    out = lax.map(one, (qb, jnp.arange(nb)))
    return out.swapaxes(0, 1).reshape(B, S, A_HEADS, 2 * A_DH)


def pool_mix(u, hist, n_hist, w, scale):
    B, S, _ = u.shape
    ext = jnp.concatenate([hist, u], axis=1).astype(jnp.float32)
    cs = jnp.concatenate([jnp.zeros_like(ext[:, :1]), jnp.cumsum(ext, axis=1)], axis=1)
    end = cs[:, POOL_HIST + 1:]
    t = jnp.arange(S, dtype=jnp.float32)
    outs = []
    for g, win in enumerate(POOL_WINDOWS):
        sl = slice(g * POOL_CH, (g + 1) * POOL_CH)
        start = cs[:, POOL_HIST + 1 - win: POOL_HIST + 1 - win + S, sl]
        cnt = jnp.minimum(float(win), t + 1.0 + n_hist)[None, :, None]
        outs.append((end[..., sl] - start) / cnt)
    pooled = (jnp.concatenate(outs, axis=-1) - u.astype(jnp.float32)).astype(u.dtype)
    y = jnp.einsum('bsgc,gcd->bsgd', pooled.reshape(B, S, POOL_GROUPS, POOL_CH), w)
    return y.reshape(B, S, B_WIDTH) * scale, ext[:, -POOL_HIST:].astype(u.dtype)


def even_mixer(h, pos, k_cache, v_cache, pool_hist, n_hist, w_in, w_out, lam_p, subln, pool_w, pool_scale, li):
    B, S, _ = h.shape
    z = h @ w_in
    q, k, v, u = jnp.split(z, [A_WIDTH, 2 * A_WIDTH, 3 * A_WIDTH], axis=-1)
    q = rope(q.reshape(B, S, A_HEADS, 2, A_DH), pos)
    k = rope(k.reshape(B, S, A_HEADS, 2, A_DH), pos)
    v = v.reshape(B, S, A_HEADS, 2 * A_DH)
    lam_init = 0.8 - 0.6 * math.exp(-0.3 * li)
    lp = lam_p.astype(jnp.float32)
    lam = jnp.exp(jnp.sum(lp[0] * lp[1])) - jnp.exp(jnp.sum(lp[2] * lp[3])) + lam_init
    if k_cache is None:
        a = diff_attn_prompt(q, k, v, lam)
    else:
        k_all = jnp.concatenate([k_cache, k], axis=1)
        v_all = jnp.concatenate([v_cache, v], axis=1)
        a = diff_core(q, k_all, v_all, lam, None)
    a = (rmsnorm(a, subln) * (1.0 - lam_init)).reshape(B, S, A_WIDTH)
    p, new_pool = pool_mix(u, pool_hist, n_hist, pool_w, pool_scale)
    out = jnp.concatenate([a, p], axis=-1) @ w_out
    return out, k, v, new_pool


def sink_attend(q, k, v, sink, mask):
    s = jnp.einsum('...qkgd,...jkd->...kgqj', q, k).astype(jnp.float32) * (C_DH ** -0.5)
    if mask is not None:
        s = jnp.where(mask, s, -jnp.inf)
    sk = jnp.broadcast_to(sink.astype(jnp.float32)[:, :, None, None], s.shape[:-1] + (1,))
    p = jax.nn.softmax(jnp.concatenate([s, sk], axis=-1), axis=-1)[..., :-1]
    return jnp.einsum('...kgqj,...jkd->...qkgd', p.astype(v.dtype), v)


def swa_prompt(q, k, v, sink):
    B, S = q.shape[:2]
    nC = S // CHUNK
    qc = q.reshape(B, nC, CHUNK, C_KV, C_GROUP, C_DH)
    pad = WIN_CHUNKS * CHUNK

    def band(t):
        tp = jnp.pad(t, ((0, 0), (pad, 0), (0, 0), (0, 0))).reshape(B, nC + WIN_CHUNKS, CHUNK, C_KV, C_DH)
        return jnp.concatenate([tp[:, i:i + nC] for i in range(WIN_CHUNKS + 1)], axis=2)

    kb, vb = band(k), band(v)
    key_pos = (jnp.arange(nC)[:, None] - WIN_CHUNKS) * CHUNK + jnp.arange((WIN_CHUNKS + 1) * CHUNK)[None, :]
    mask = (key_pos >= 0)[:, None, None, None, :]
    out = sink_attend(qc, kb, vb, sink, mask)
    return out.reshape(B, S, C_HEADS * C_DH)


def odd_mixer(h, pos, k_cache, v_cache, w_in, b_in, w_out, sink):
    B, S, _ = h.shape
    z = h @ w_in + b_in
    q, k, v = jnp.split(z, [C_HEADS * C_DH, C_HEADS * C_DH + C_KV * C_DH], axis=-1)
    q = rope(q.reshape(B, S, C_KV, C_GROUP, C_DH), pos)
    k = rope(k.reshape(B, S, C_KV, C_DH), pos)
    v = v.reshape(B, S, C_KV, C_DH)
    sink = sink.reshape(C_KV, C_GROUP)
    if k_cache is None:
        o = swa_prompt(q, k, v, sink)
        k_all, v_all = k, v
    else:
        k_all = jnp.concatenate([k_cache, k], axis=1)
        v_all = jnp.concatenate([v_cache, v], axis=1)
        o = sink_attend(q, k_all, v_all, sink, None).reshape(B, S, C_HEADS * C_DH)
    return o @ w_out, k_all[:, -C_CACHE:], v_all[:, -C_CACHE:]


def conv_ffn(x, hist, w_up, conv_w, conv_b, w_down):
    S = x.shape[1]
    up = x @ w_up
    ext = jnp.concatenate([hist, up], axis=1)
    c = conv_b + ext[:, 0:S] * conv_w[0]
    for j in range(1, CONV_W):
        c = c + ext[:, j:j + S] * conv_w[j]
    gate, val = jnp.split(c, 2, axis=-1)
    return (jax.nn.silu(gate) * val) @ w_down, ext[:, -(CONV_W - 1):]


def setup_inputs(seed: int = 0) -> dict:
    key = jax.random.key(seed)
    ks = jax.random.split(key, 26)
    D = D_MODEL

    def nrm(k, shape, s=1.0):
        return s * jax.random.normal(k, shape, jnp.float32)

    return {
        'x_prompt': nrm(ks[0], (BATCH, SEQ, D)),
        'x_sample': nrm(ks[1], (DEC_BATCH, DEC_SEQ, D)),
        'cache_diff_k': nrm(ks[2], (N_EVEN, DEC_BATCH, PAST_LEN, A_HEADS, 2, A_DH)),
        'cache_diff_v': nrm(ks[3], (N_EVEN, DEC_BATCH, PAST_LEN, A_HEADS, 2 * A_DH)),
        'state_pool': nrm(ks[4], (N_EVEN, DEC_BATCH, POOL_HIST, B_WIDTH)),
        'cache_swa_k': nrm(ks[5], (N_ODD, DEC_BATCH, C_CACHE, C_KV, C_DH)),
        'cache_swa_v': nrm(ks[6], (N_ODD, DEC_BATCH, C_CACHE, C_KV, C_DH)),
        'state_ffn_conv': nrm(ks[7], (DEPTH, DEC_BATCH, CONV_W - 1, 2 * D_FF)),
        'norm_attn': 1.0 + nrm(ks[8], (DEPTH, D), 0.05),
        'norm_ffn': 1.0 + nrm(ks[9], (DEPTH, D), 0.05),
        'norm_final': 1.0 + nrm(ks[10], (D,), 0.05),
        'w_in_even': nrm(ks[11], (N_EVEN, D, EVEN_IN), D ** -0.5),
        'w_out_even': nrm(ks[12], (N_EVEN, A_WIDTH + B_WIDTH, D), (A_WIDTH + B_WIDTH) ** -0.5),
        'diff_lambda': nrm(ks[13], (N_EVEN, 4, A_DH), 0.1),
        'diff_subln': 1.0 + nrm(ks[14], (N_EVEN, 2 * A_DH), 0.05),
        'pool_w': nrm(ks[15], (N_EVEN, POOL_GROUPS, POOL_CH, POOL_CH), POOL_CH ** -0.5),
        'pool_scale': 1.0 + nrm(ks[16], (N_EVEN, B_WIDTH), 0.1),
        'w_in_odd': nrm(ks[17], (N_ODD, D, ODD_IN), D ** -0.5),
        'b_in_odd': nrm(ks[18], (N_ODD, ODD_IN), 0.02),
        'w_out_odd': nrm(ks[19], (N_ODD, C_HEADS * C_DH, D), (C_HEADS * C_DH) ** -0.5),
        'sinks': nrm(ks[20], (N_ODD, C_HEADS), 0.5),
        'w_up': nrm(ks[21], (DEPTH, D, 2 * D_FF), D ** -0.5),
        'conv_w': nrm(ks[22], (DEPTH, CONV_W, 2 * D_FF), CONV_W ** -0.5),
        'conv_b': nrm(ks[23], (DEPTH, 2 * D_FF), 0.02),
        'w_down': nrm(ks[24], (DEPTH, D_FF, D), D_FF ** -0.5),
    }


def reference(x_prompt, x_sample, cache_diff_k, cache_diff_v, state_pool, cache_swa_k, cache_swa_v, state_ffn_conv,
              norm_attn, norm_ffn, norm_final, w_in_even, w_out_even, diff_lambda, diff_subln, pool_w, pool_scale,
              w_in_odd, b_in_odd, w_out_odd, sinks, w_up, conv_w, conv_b, w_down):
    hp, hs = x_prompt, x_sample
    bp, sp = x_prompt.shape[:2]
    pos_p = jnp.arange(sp, dtype=jnp.int32)
    pos_s = PAST_LEN + jnp.arange(x_sample.shape[1], dtype=jnp.int32)
    dkp, dvp, plp, skp, svp, fcp = [], [], [], [], [], []
    dks, dvs, pls, sks, svs, fcs = [], [], [], [], [], []
    for i in range(DEPTH):
        j = i // 2
        hn_p = rmsnorm(hp, norm_attn[i])
        hn_s = rmsnorm(hs, norm_attn[i])
        if i % 2 == 0:
            ew = (w_in_even[j], w_out_even[j], diff_lambda[j], diff_subln[j], pool_w[j], pool_scale[j], i)
            zero_pool = jnp.zeros((bp, POOL_HIST, B_WIDTH), hp.dtype)
            mp, kp, vp, pp = even_mixer(hn_p, pos_p, None, None, zero_pool, 0, *ew)
            ms, k_s, v_s, ps = even_mixer(hn_s, pos_s, cache_diff_k[j], cache_diff_v[j], state_pool[j], POOL_HIST, *ew)
            dkp.append(kp); dvp.append(vp); plp.append(pp)
            dks.append(k_s); dvs.append(v_s); pls.append(ps)
        else:
            ow = (w_in_odd[j], b_in_odd[j], w_out_odd[j], sinks[j])
            mp, kp, vp = odd_mixer(hn_p, pos_p, None, None, *ow)
            ms, k_s, v_s = odd_mixer(hn_s, pos_s, cache_swa_k[j], cache_swa_v[j], *ow)
            skp.append(kp); svp.append(vp)
            sks.append(k_s); svs.append(v_s)
        hp = hp + mp
        hs = hs + ms
        fw = (w_up[i], conv_w[i], conv_b[i], w_down[i])
        zero_conv = jnp.zeros((bp, CONV_W - 1, 2 * D_FF), hp.dtype)
        fp, cp = conv_ffn(rmsnorm(hp, norm_ffn[i]), zero_conv, *fw)
        f_s, c_s = conv_ffn(rmsnorm(hs, norm_ffn[i]), state_ffn_conv[i], *fw)
        hp = hp + fp
        hs = hs + f_s
        fcp.append(cp); fcs.append(c_s)
    y_prompt = rmsnorm(hp, norm_final)
    y_sample = rmsnorm(hs, norm_final)
    diff_k_prompt = jnp.stack(dkp)
    diff_v_prompt = jnp.stack(dvp)
    pool_prompt = jnp.stack(plp)
    swa_k_prompt = jnp.stack(skp)
    swa_v_prompt = jnp.stack(svp)
    ffn_conv_prompt = jnp.stack(fcp)
    diff_k_sample = jnp.stack(dks)
    diff_v_sample = jnp.stack(dvs)
    pool_sample = jnp.stack(pls)
    swa_k_sample = jnp.stack(sks)
    swa_v_sample = jnp.stack(svs)
    ffn_conv_sample = jnp.stack(fcs)
    return (y_prompt, y_sample, diff_k_prompt, diff_v_prompt, pool_prompt, swa_k_prompt, swa_v_prompt, ffn_conv_prompt,
            diff_k_sample, diff_v_sample, pool_sample, swa_k_sample, swa_v_sample, ffn_conv_sample)
```

```python
import functools
import math

import jax
import jax.numpy as jnp
from jax import lax
from jax.experimental import pallas as pl
from jax.experimental.pallas import tpu as pltpu

F32 = jnp.float32
BF16 = jnp.bfloat16

D_MODEL = 1024
CHUNK = 64
ROPE_THETA = 10000.0
EPS = 1e-5
A_HEADS = 4
A_DH = 64
A_WIDTH = A_HEADS * 2 * A_DH
POOL_WINDOWS = (2, 4, 8, 16)
POOL_CH = 128
POOL_HIST = 15
B_WIDTH = 512
C_HEADS = 16
C_KV = 2
C_DH = 64
C_CACHE = 128
D_FF = 2816
CONV_W = 3

LANES = 128
HIST_ROWS = 16
CONV_PAD = 8
FF_CHUNK = 256
N_FF_CHUNKS = D_FF // FF_CHUNK
ROW_TILE = 512
ATTN_TILE = 512
SWA_TILE = 2 * CHUNK
VMEM_LIMIT = 56 * 1024 * 1024


def _params(n_axes, vmem=VMEM_LIMIT):
    return pltpu.CompilerParams(dimension_semantics=("arbitrary",) * n_axes, vmem_limit_bytes=vmem)


def _rmsnorm_rows(x, g):
    return x * lax.rsqrt(jnp.mean(x * x, axis=-1, keepdims=True) + EPS) * g


def _rope128(z, cos, s_up, s_dn):
    return z * cos + pltpu.roll(z, 96, 1) * s_up + pltpu.roll(z, 32, 1) * s_dn


def _rope_tables(pos):
    inv = ROPE_THETA ** (-jnp.arange(0, A_DH, 2, dtype=F32) / A_DH)
    ang = pos.astype(F32)[:, None] * inv[None, :]
    cos, sin = jnp.cos(ang), jnp.sin(ang)
    zero = jnp.zeros_like(sin)
    cos128 = jnp.tile(cos, (1, 4))
    s_up = jnp.tile(jnp.concatenate([-sin, zero], axis=1), (1, 2))
    s_dn = jnp.tile(jnp.concatenate([zero, sin], axis=1), (1, 2))
    return cos128, s_up, s_dn


def _even_in_kernel(x_ref, g_ref, w_ref, cos_ref, sup_ref, sdn_ref,
                    q_ref, k_ref, kz_ref, v_ref, vb_ref, u_ref):
    tm = x_ref.shape[0]
    hn = _rmsnorm_rows(x_ref[...], g_ref[...]).astype(BF16)
    z = jnp.dot(hn, w_ref[...], preferred_element_type=F32)
    cos, s_up, s_dn = cos_ref[...], sup_ref[...], sdn_ref[...]
    lo = lax.broadcasted_iota(jnp.int32, (tm, LANES), 1) < A_DH
    for c in range(A_HEADS):
        sl = slice(c * LANES, (c + 1) * LANES)
        rq = _rope128(z[:, sl], cos, s_up, s_dn)
        q_ref[:, sl] = (rq * (A_DH ** -0.5)).astype(BF16)
        rk = _rope128(z[:, A_WIDTH + c * LANES:A_WIDTH + (c + 1) * LANES], cos, s_up, s_dn)
        k_ref[:, sl] = rk
        kz_ref[0, :, sl] = jnp.where(lo, rk, 0.0).astype(BF16)
        kz_ref[1, :, sl] = jnp.where(lo, 0.0, rk).astype(BF16)
    v = z[:, 2 * A_WIDTH:3 * A_WIDTH]
    v_ref[...] = v
    vb_ref[...] = v.astype(BF16)
    u_ref[...] = z[:, 3 * A_WIDTH:]


def _even_in(x, g, w, tabs, tm):
    t = x.shape[0]
    n_tab = tabs[0].shape[0] // tm
    row = lambda i: (i, 0)
    tab = lambda i: (i % n_tab, 0)
    const = lambda i: (0, 0)
    return pl.pallas_call(
        _even_in_kernel,
        grid=(t // tm,),
        in_specs=[pl.BlockSpec((tm, D_MODEL), row), pl.BlockSpec((1, D_MODEL), const),
                  pl.BlockSpec(w.shape, const),
                  pl.BlockSpec((tm, LANES), tab), pl.BlockSpec((tm, LANES), tab), pl.BlockSpec((tm, LANES), tab)],
        out_specs=[pl.BlockSpec((tm, A_WIDTH), row), pl.BlockSpec((tm, A_WIDTH), row),
                   pl.BlockSpec((2, tm, A_WIDTH), lambda i: (0, i, 0)),
                   pl.BlockSpec((tm, A_WIDTH), row), pl.BlockSpec((tm, A_WIDTH), row),
                   pl.BlockSpec((tm, B_WIDTH), row)],
        out_shape=[jax.ShapeDtypeStruct((t, A_WIDTH), BF16), jax.ShapeDtypeStruct((t, A_WIDTH), F32),
                   jax.ShapeDtypeStruct((2, t, A_WIDTH), BF16),
                   jax.ShapeDtypeStruct((t, A_WIDTH), F32), jax.ShapeDtypeStruct((t, A_WIDTH), BF16),
                   jax.ShapeDtypeStruct((t, B_WIDTH), F32)],
        compiler_params=_params(1), name="even_in",
    )(x, g, w, *tabs)


def _odd_in_kernel(x_ref, g_ref, w_ref, b_ref, cos_ref, sup_ref, sdn_ref,
                   q_ref, k_ref, v_ref, kk_ref, vv_ref):
    tm = x_ref.shape[0]
    hn = _rmsnorm_rows(x_ref[...], g_ref[...]).astype(BF16)
    z = jnp.dot(hn, w_ref[...], preferred_element_type=F32) + b_ref[...]
    cos, s_up, s_dn = cos_ref[...], sup_ref[...], sdn_ref[...]
    nq = C_HEADS * C_DH
    for c in range(nq // LANES):
        sl = slice(c * LANES, (c + 1) * LANES)
        q_ref[:, sl] = (_rope128(z[:, sl], cos, s_up, s_dn) * (C_DH ** -0.5)).astype(BF16)
    k = _rope128(z[:, nq:nq + LANES], cos, s_up, s_dn)
    v = z[:, nq + LANES:nq + 2 * LANES]
    k_ref[...] = k
    v_ref[...] = v
    lo = lax.broadcasted_iota(jnp.int32, (tm, LANES), 1) < C_DH
    for src, dst in ((k, kk_ref), (v, vv_ref)):
        h0 = jnp.where(lo, src, 0.0)
        h1 = jnp.where(lo, 0.0, src)
        dst[0] = h0.astype(BF16)
        dst[1] = pltpu.roll(h0, C_DH, 1).astype(BF16)
        dst[2] = pltpu.roll(h1, C_DH, 1).astype(BF16)
        dst[3] = h1.astype(BF16)


def _odd_in(x, g, w, b, tabs, tm):
    t = x.shape[0]
    n_tab = tabs[0].shape[0] // tm
    nq = C_HEADS * C_DH
    row = lambda i: (i, 0)
    tab = lambda i: (i % n_tab, 0)
    const = lambda i: (0, 0)
    return pl.pallas_call(
        _odd_in_kernel,
        grid=(t // tm,),
        in_specs=[pl.BlockSpec((tm, D_MODEL), row), pl.BlockSpec((1, D_MODEL), const),
                  pl.BlockSpec(w.shape, const), pl.BlockSpec((1, w.shape[1]), const),
                  pl.BlockSpec((tm, LANES), tab), pl.BlockSpec((tm, LANES), tab), pl.BlockSpec((tm, LANES), tab)],
        out_specs=[pl.BlockSpec((tm, nq), row), pl.BlockSpec((tm, LANES), row), pl.BlockSpec((tm, LANES), row),
                   pl.BlockSpec((4, tm, LANES), lambda i: (0, i, 0)),
                   pl.BlockSpec((4, tm, LANES), lambda i: (0, i, 0))],
        out_shape=[jax.ShapeDtypeStruct((t, nq), BF16), jax.ShapeDtypeStruct((t, LANES), F32),
                   jax.ShapeDtypeStruct((t, LANES), F32),
                   jax.ShapeDtypeStruct((4, t, LANES), BF16), jax.ShapeDtypeStruct((4, t, LANES), BF16)],
        compiler_params=_params(1), name="odd_in",
    )(x, g, w, b, *tabs)


def _diff_lambda(lam_ref, lam_init):
    lp = lam_ref[...]
    return (jnp.exp(jnp.sum(lp[0:1] * lp[1:2], axis=-1, keepdims=True))
            - jnp.exp(jnp.sum(lp[2:3] * lp[3:4], axis=-1, keepdims=True)) + lam_init)


def _diff_finish(o1, o2, lam, sub, lam_init):
    a = o1 - lam * o2
    return _rmsnorm_rows(a, sub) * (1.0 - lam_init)


def _diff_prompt_kernel(q_ref, kz_ref, v_ref, lam_ref, sub_ref, o_ref, m_ref, l_ref, acc_ref, *, lam_init):
    i = pl.program_id(2)
    tq = q_ref.shape[0]
    tk = tq
    q = q_ref[...]
    m_ref[...] = jnp.full(m_ref.shape, -jnp.inf, F32)
    l_ref[...] = jnp.zeros(l_ref.shape, F32)
    acc_ref[...] = jnp.zeros(acc_ref.shape, F32)

    def tile(j, mask):
        start = pl.multiple_of(j * tk, tk)
        v = v_ref[pl.ds(start, tk), :]
        for mp in range(2):
            s = lax.dot_general(q, kz_ref[mp, pl.ds(start, tk), :], (((1,), (1,)), ((), ())),
                                preferred_element_type=F32)
            if mask is not None:
                s = jnp.where(mask, s, -jnp.inf)
            m_prev = m_ref[mp]
            m_new = jnp.maximum(m_prev, jnp.max(s, axis=-1, keepdims=True))
            alpha = jnp.exp(m_prev - m_new)
            p = jnp.exp(s - m_new)
            l_ref[mp] = alpha * l_ref[mp] + jnp.sum(p, axis=-1, keepdims=True)
            acc_ref[mp] = alpha * acc_ref[mp] + jnp.dot(p.astype(BF16), v, preferred_element_type=F32)
            m_ref[mp] = m_new

    def body(j, carry):
        tile(j, None)
        return carry

    lax.fori_loop(0, i, body, 0)
    rows = lax.broadcasted_iota(jnp.int32, (tq, tk), 0) // CHUNK
    cols = lax.broadcasted_iota(jnp.int32, (tq, tk), 1) // CHUNK
    tile(i, cols <= rows)
    lam = _diff_lambda(lam_ref, lam_init)
    o1 = acc_ref[0] / l_ref[0]
    o2 = acc_ref[1] / l_ref[1]
    o_ref[...] = _diff_finish(o1, o2, lam, sub_ref[...], lam_init).astype(BF16)


def _diff_prompt(q, kz, vb, lam_p, sub, batch, seq, lam_init):
    t = q.shape[0]
    tq = ATTN_TILE
    nq = seq // tq
    return pl.pallas_call(
        functools.partial(_diff_prompt_kernel, lam_init=lam_init),
        grid=(batch, A_HEADS, nq),
        in_specs=[pl.BlockSpec((tq, LANES), lambda b, h, i: (b * nq + i, h)),
                  pl.BlockSpec((2, seq, LANES), lambda b, h, i: (0, b, h)),
                  pl.BlockSpec((seq, LANES), lambda b, h, i: (b, h)),
                  pl.BlockSpec((4, A_DH), lambda b, h, i: (0, 0)),
                  pl.BlockSpec((1, LANES), lambda b, h, i: (0, 0))],
        out_specs=pl.BlockSpec((tq, LANES), lambda b, h, i: (b * nq + i, h)),
        out_shape=jax.ShapeDtypeStruct((t, A_WIDTH), BF16),
        scratch_shapes=[pltpu.VMEM((2, tq, 1), F32), pltpu.VMEM((2, tq, 1), F32),
                        pltpu.VMEM((2, tq, LANES), F32)],
        compiler_params=_params(3), name="diff_attn_prompt",
    )(q, kz, vb, lam_p, sub)


def _diff_sample_kernel(q_ref, kz_ref, v_ref, ck_ref, cv_ref, lam_ref, sub_ref, o_ref, *, lam_init):
    past = ck_ref.shape[1]
    lo = lax.broadcasted_iota(jnp.int32, (past, LANES), 1) < A_DH
    lam = _diff_lambda(lam_ref, lam_init)
    nt = (((1,), (1,)), ((), ()))
    for h in range(A_HEADS):
        sl = slice(h * LANES, (h + 1) * LANES)
        q = q_ref[:, sl]
        ck = ck_ref[0, :, sl]
        cv = cv_ref[0, :, sl].astype(BF16)
        vn = v_ref[:, sl]
        outs = []
        for mp in range(2):
            ckm = (jnp.where(lo, ck, 0.0) if mp == 0 else jnp.where(lo, 0.0, ck)).astype(BF16)
            s_c = lax.dot_general(q, ckm, nt, preferred_element_type=F32)
            s_n = lax.dot_general(q, kz_ref[mp, :, sl], nt, preferred_element_type=F32)
            m = jnp.maximum(jnp.max(s_c, axis=-1, keepdims=True), jnp.max(s_n, axis=-1, keepdims=True))
            e_c = jnp.exp(s_c - m)
            e_n = jnp.exp(s_n - m)
            den = jnp.sum(e_c, axis=-1, keepdims=True) + jnp.sum(e_n, axis=-1, keepdims=True)
            o = (jnp.dot(e_c.astype(BF16), cv, preferred_element_type=F32)
                 + jnp.dot(e_n.astype(BF16), vn, preferred_element_type=F32))
            outs.append(o / den)
        o_ref[:, sl] = _diff_finish(outs[0], outs[1], lam, sub_ref[...], lam_init).astype(BF16)


def _diff_sample(q, kz, vb, cache_k, cache_v, lam_p, sub, lam_init):
    nb, past, _ = cache_k.shape
    t = q.shape[0]
    sq = t // nb
    return pl.pallas_call(
        functools.partial(_diff_sample_kernel, lam_init=lam_init),
        grid=(nb,),
        in_specs=[pl.BlockSpec((sq, A_WIDTH), lambda b: (b, 0)),
                  pl.BlockSpec((2, sq, A_WIDTH), lambda b: (0, b, 0)),
                  pl.BlockSpec((sq, A_WIDTH), lambda b: (b, 0)),
                  pl.BlockSpec((1, past, A_WIDTH), lambda b: (b, 0, 0)),
                  pl.BlockSpec((1, past, A_WIDTH), lambda b: (b, 0, 0)),
                  pl.BlockSpec((4, A_DH), lambda b: (0, 0)),
                  pl.BlockSpec((1, LANES), lambda b: (0, 0))],
        out_specs=pl.BlockSpec((sq, A_WIDTH), lambda b: (b, 0)),
        out_shape=jax.ShapeDtypeStruct((t, A_WIDTH), BF16),
        compiler_params=_params(1), name="diff_attn_sample",
    )(q, kz, vb, cache_k, cache_v, lam_p, sub)


def _even_out_kernel(a_ref, u_ref, hist_ref, x_ref, pw_ref, ps_ref, w_ref, o_ref, ext_ref, *, n_hist):
    i = pl.program_id(1)
    tm = u_ref.shape[0]

    @pl.when(i == 0)
    def _():
        ext_ref[0:HIST_ROWS, :] = hist_ref[0]

    u = u_ref[...]
    ext_ref[HIST_ROWS:HIST_ROWS + tm, :] = u
    pos = (i * tm + lax.broadcasted_iota(jnp.int32, (tm, 1), 0)).astype(F32)
    acc = x_ref[...] + jnp.dot(a_ref[...], w_ref[0:A_WIDTH, :], preferred_element_type=F32)
    for g, win in enumerate(POOL_WINDOWS):
        sl = slice(g * POOL_CH, (g + 1) * POOL_CH)
        tot = u[:, sl]
        for back in range(1, win):
            tot = tot + ext_ref[HIST_ROWS - back:HIST_ROWS - back + tm, sl]
        cnt = jnp.minimum(float(win), pos + (1.0 + n_hist))
        pooled = (tot / cnt - u[:, sl]).astype(BF16)
        y = jnp.dot(pooled, pw_ref[g], preferred_element_type=F32) * ps_ref[:, sl]
        acc = acc + jnp.dot(y.astype(BF16), w_ref[A_WIDTH + g * POOL_CH:A_WIDTH + (g + 1) * POOL_CH, :],
                            preferred_element_type=F32)
    o_ref[...] = acc
    ext_ref[0:HIST_ROWS, :] = ext_ref[tm:tm + HIST_ROWS, :]


def _even_out(a, u, hist, x, pool_w, pool_scale, w_out, nseq, tm, n_hist):
    t = x.shape[0]
    nt = t // nseq // tm
    row = lambda b, i: (b * nt + i, 0)
    return pl.pallas_call(
        functools.partial(_even_out_kernel, n_hist=n_hist),
        grid=(nseq, nt),
        in_specs=[pl.BlockSpec((tm, A_WIDTH), row), pl.BlockSpec((tm, B_WIDTH), row),
                  pl.BlockSpec((1, HIST_ROWS, B_WIDTH), lambda b, i: (b, 0, 0)),
                  pl.BlockSpec((tm, D_MODEL), row),
                  pl.BlockSpec(pool_w.shape, lambda b, i: (0, 0, 0)),
                  pl.BlockSpec((1, B_WIDTH), lambda b, i: (0, 0)),
                  pl.BlockSpec(w_out.shape, lambda b, i: (0, 0))],
        out_specs=pl.BlockSpec((tm, D_MODEL), row),
        out_shape=jax.ShapeDtypeStruct((t, D_MODEL), F32),
        scratch_shapes=[pltpu.VMEM((HIST_ROWS + tm, B_WIDTH), F32)],
        compiler_params=_params(2), name="even_out",
    )(a, u, hist, x, pool_w, pool_scale, w_out)


def _sink_softmax(s, sink):
    m = jnp.maximum(jnp.max(s, axis=-1, keepdims=True), sink)
    e = jnp.exp(s - m)
    return e / (jnp.sum(e, axis=-1, keepdims=True) + jnp.exp(sink - m))


def _sink_column(sink_ref, kv, half, rows):
    n = rows // 4
    return jnp.concatenate(
        [jnp.full((n, 1), sink_ref[kv * 8 + 2 * p + half], F32) for p in range(4)], axis=0)


def _swa_prompt_kernel(sink_ref, q_ref, kp_ref, kc_ref, vp_ref, vc_ref, o_ref):
    i = pl.program_id(1)
    tq = q_ref.shape[0]
    nk = 2 * tq
    r = lax.broadcasted_iota(jnp.int32, (4 * tq, 2 * nk), 0) % tq // CHUNK
    c = lax.broadcasted_iota(jnp.int32, (4 * tq, 2 * nk), 1) % nk // CHUNK
    vis = (c >= r) & (c <= r + 2) & ((c >= 2) | (i > 0))
    for kv in range(C_KV):
        qs = jnp.concatenate([q_ref[:, (kv * 4 + p) * LANES:(kv * 4 + p + 1) * LANES] for p in range(4)], axis=0)
        ks = jnp.concatenate([kp_ref[2 * kv], kc_ref[2 * kv], kp_ref[2 * kv + 1], kc_ref[2 * kv + 1]], axis=0)
        vs = jnp.concatenate([vp_ref[2 * kv], vc_ref[2 * kv], vp_ref[2 * kv + 1], vc_ref[2 * kv + 1]], axis=0)
        s = lax.dot_general(qs, ks, (((1,), (1,)), ((), ())), preferred_element_type=F32)
        s = jnp.where(vis, s, -jnp.inf)
        p = jnp.concatenate(
            [_sink_softmax(s[:, half * nk:(half + 1) * nk], _sink_column(sink_ref, kv, half, 4 * tq))
             for half in range(2)], axis=1)
        o = jnp.dot(p.astype(BF16), vs, preferred_element_type=F32)
        for pr in range(4):
            o_ref[:, (kv * 4 + pr) * LANES:(kv * 4 + pr + 1) * LANES] = o[pr * tq:(pr + 1) * tq].astype(BF16)


def _swa_prompt(sinks, q, kk, vv, batch, seq):
    t = q.shape[0]
    tq = SWA_TILE
    nt = seq // tq
    prev = lambda b, i: (0, b * nt + jnp.maximum(i - 1, 0), 0)
    cur = lambda b, i: (0, b * nt + i, 0)
    kvspec = lambda f: pl.BlockSpec((4, tq, LANES), f)
    return pl.pallas_call(
        _swa_prompt_kernel,
        grid=(batch, nt),
        in_specs=[pl.BlockSpec(memory_space=pltpu.SMEM),
                  pl.BlockSpec((tq, C_HEADS * C_DH), lambda b, i: (b * nt + i, 0)),
                  kvspec(prev), kvspec(cur), kvspec(prev), kvspec(cur)],
        out_specs=pl.BlockSpec((tq, C_HEADS * C_DH), lambda b, i: (b * nt + i, 0)),
        out_shape=jax.ShapeDtypeStruct((t, C_HEADS * C_DH), BF16),
        compiler_params=_params(2), name="swa_prompt",
    )(sinks, q, kk, kk, vv, vv)


def _swa_sample_kernel(sink_ref, q_ref, kk_ref, vv_ref, ck_ref, cv_ref, o_ref):
    sq = q_ref.shape[0]
    nc = ck_ref.shape[1]
    lo = lax.broadcasted_iota(jnp.int32, (nc, LANES), 1) < C_DH
    nt = (((1,), (1,)), ((), ()))

    def halves(c, kv):
        own = jnp.where(lo, c, 0.0) if kv == 0 else jnp.where(lo, 0.0, c)
        swapped = pltpu.roll(own, C_DH, 1)
        return ((own, swapped) if kv == 0 else (swapped, own))

    ck = ck_ref[0]
    cv = cv_ref[0]
    for kv in range(C_KV):
        qs = jnp.concatenate([q_ref[:, (kv * 4 + p) * LANES:(kv * 4 + p + 1) * LANES] for p in range(4)], axis=0)
        ckh = halves(ck, kv)
        cvh = halves(cv, kv)
        o = jnp.zeros((4 * sq, LANES), F32)
        for half in range(2):
            ks = jnp.concatenate([ckh[half].astype(BF16), kk_ref[2 * kv + half]], axis=0)
            vs = jnp.concatenate([cvh[half].astype(BF16), vv_ref[2 * kv + half]], axis=0)
            s = lax.dot_general(qs, ks, nt, preferred_element_type=F32)
            p = _sink_softmax(s, _sink_column(sink_ref, kv, half, 4 * sq))
            o = o + jnp.dot(p.astype(BF16), vs, preferred_element_type=F32)
        for pr in range(4):
            o_ref[:, (kv * 4 + pr) * LANES:(kv * 4 + pr + 1) * LANES] = o[pr * sq:(pr + 1) * sq].astype(BF16)


def _swa_sample(sinks, q, kk, vv, cache_k, cache_v):
    nb, nc, _ = cache_k.shape
    t = q.shape[0]
    sq = t // nb
    return pl.pallas_call(
        _swa_sample_kernel,
        grid=(nb,),
        in_specs=[pl.BlockSpec(memory_space=pltpu.SMEM),
                  pl.BlockSpec((sq, C_HEADS * C_DH), lambda b: (b, 0)),
                  pl.BlockSpec((4, sq, LANES), lambda b: (0, b, 0)),
                  pl.BlockSpec((4, sq, LANES), lambda b: (0, b, 0)),
                  pl.BlockSpec((1, nc, LANES), lambda b: (b, 0, 0)),
                  pl.BlockSpec((1, nc, LANES), lambda b: (b, 0, 0))],
        out_specs=pl.BlockSpec((sq, C_HEADS * C_DH), lambda b: (b, 0)),
        out_shape=jax.ShapeDtypeStruct((t, C_HEADS * C_DH), BF16),
        compiler_params=_params(1), name="swa_sample",
    )(sinks, q, kk, vv, cache_k, cache_v)


def _odd_out_kernel(o_ref, x_ref, w_ref, y_ref):
    y_ref[...] = x_ref[...] + jnp.dot(o_ref[...], w_ref[...], preferred_element_type=F32)


def _odd_out(o, x, w, tm):
    t = x.shape[0]
    row = lambda i: (i, 0)
    return pl.pallas_call(
        _odd_out_kernel,
        grid=(t // tm,),
        in_specs=[pl.BlockSpec((tm, o.shape[1]), row), pl.BlockSpec((tm, D_MODEL), row),
                  pl.BlockSpec(w.shape, lambda i: (0, 0))],
        out_specs=pl.BlockSpec((tm, D_MODEL), row),
        out_shape=jax.ShapeDtypeStruct((t, D_MODEL), F32),
        compiler_params=_params(1), name="odd_out",
    )(o, x, w)


def _ffn_kernel(x_ref, g_ref, hist_ref, wu_ref, cw_ref, cb_ref, wd_ref, gf_ref,
                y_ref, st_ref, hn_ref, ext_ref, carry_ref, acc_ref, *, final_norm):
    i = pl.program_id(1)
    tm = x_ref.shape[0]

    @pl.when(i == 0)
    def _():
        carry_ref[...] = hist_ref[0]

    x = x_ref[...]
    hn_ref[...] = _rmsnorm_rows(x, g_ref[...]).astype(BF16)
    acc_ref[...] = x

    def chunk(c, carry):
        up = jnp.dot(hn_ref[...], wu_ref[c], preferred_element_type=F32)
        ext_ref[0:CONV_PAD, :] = carry_ref[c]
        ext_ref[CONV_PAD:CONV_PAD + tm, :] = up
        cw = cw_ref[c]
        conv = (cb_ref[c] + ext_ref[CONV_PAD - 2:CONV_PAD - 2 + tm, :] * cw[0:1]
                + ext_ref[CONV_PAD - 1:CONV_PAD - 1 + tm, :] * cw[1:2] + up * cw[2:3])
        carry_ref[c] = ext_ref[tm:tm + CONV_PAD, :]
        gate = conv[:, :FF_CHUNK]
        h = (gate * jax.nn.sigmoid(gate) * conv[:, FF_CHUNK:]).astype(BF16)
        acc_ref[...] += jnp.dot(h, wd_ref[c], preferred_element_type=F32)
        return carry

    lax.fori_loop(0, N_FF_CHUNKS, chunk, 0)
    y = acc_ref[...]
    if final_norm:
        y = _rmsnorm_rows(y, gf_ref[...])
    y_ref[...] = y
    st_ref[0] = carry_ref[...]


def _ffn(x, g, hist, wu, cw, cb, wd, g_final, nseq, tm, final_norm):
    t = x.shape[0]
    nt = t // nseq // tm
    row = lambda b, i: (b * nt + i, 0)
    c2 = lambda b, i: (0, 0)
    c3 = lambda b, i: (0, 0, 0)
    st_shape = (nseq, N_FF_CHUNKS, CONV_PAD, 2 * FF_CHUNK)
    return pl.pallas_call(
        functools.partial(_ffn_kernel, final_norm=final_norm),
        grid=(nseq, nt),
        in_specs=[pl.BlockSpec((tm, D_MODEL), row), pl.BlockSpec((1, D_MODEL), c2),
                  pl.BlockSpec((1,) + st_shape[1:], lambda b, i: (b, 0, 0, 0)),
                  pl.BlockSpec(wu.shape, c3), pl.BlockSpec(cw.shape, c3), pl.BlockSpec(cb.shape, c3),
                  pl.BlockSpec(wd.shape, c3), pl.BlockSpec((1, D_MODEL), c2)],
        out_specs=[pl.BlockSpec((tm, D_MODEL), row),
                   pl.BlockSpec((1,) + st_shape[1:], lambda b, i: (b, 0, 0, 0))],
        out_shape=[jax.ShapeDtypeStruct((t, D_MODEL), F32), jax.ShapeDtypeStruct(st_shape, F32)],
        scratch_shapes=[pltpu.VMEM((tm, D_MODEL), BF16),
                        pltpu.VMEM((CONV_PAD + tm, 2 * FF_CHUNK), F32),
                        pltpu.VMEM(st_shape[1:], F32),
                        pltpu.VMEM((tm, D_MODEL), F32)],
        compiler_params=_params(2), name="conv_ffn",
    )(x, g, hist, wu, cw, cb, wd, g_final)


def _chunk_cols(a):
    lead = a.shape[:-1]
    a = a.reshape(lead + (2, N_FF_CHUNKS, FF_CHUNK))
    a = jnp.moveaxis(a, -3, -2)
    return a.reshape(lead + (N_FF_CHUNKS, 2 * FF_CHUNK))


def _unchunk_cols(a):
    lead = a.shape[:-2]
    a = a.reshape(lead + (N_FF_CHUNKS, 2, FF_CHUNK))
    a = jnp.moveaxis(a, -2, -3)
    return a.reshape(lead + (2 * D_FF,))


def _conv_hist(state):
    b = state.shape[0]
    s = jnp.moveaxis(_chunk_cols(state), 1, 2)
    pad = jnp.zeros((b, N_FF_CHUNKS, CONV_PAD - (CONV_W - 1), 2 * FF_CHUNK), F32)
    return jnp.concatenate([pad, s], axis=2)


def _conv_state(st):
    s = st[:, :, CONV_PAD - (CONV_W - 1):, :]
    return _unchunk_cols(jnp.moveaxis(s, 2, 1))


def kernel(x_prompt, x_sample, cache_diff_k, cache_diff_v, state_pool, cache_swa_k, cache_swa_v, state_ffn_conv,
           norm_attn, norm_ffn, norm_final, w_in_even, w_out_even, diff_lambda, diff_subln, pool_w, pool_scale,
           w_in_odd, b_in_odd, w_out_odd, sinks, w_up, conv_w, conv_b, w_down):
    bp, sp, d = x_prompt.shape
    bs, ss, _ = x_sample.shape
    depth = norm_attn.shape[0]
    past = cache_diff_k.shape[2]
    hp = x_prompt.reshape(bp * sp, d)
    hs = x_sample.reshape(bs * ss, d)
    tm_s = bs * ss

    tabs_p = _rope_tables(jnp.arange(sp, dtype=jnp.int32))
    tabs_s = tuple(jnp.tile(t, (bs, 1)) for t in _rope_tables(past + jnp.arange(ss, dtype=jnp.int32)))

    wu_c = jnp.moveaxis(_chunk_cols(w_up), 2, 1).astype(BF16)
    cw_c = jnp.moveaxis(_chunk_cols(conv_w), 2, 1)
    cb_c = _chunk_cols(conv_b)[:, :, None, :]
    wd_c = w_down.reshape(depth, N_FF_CHUNKS, FF_CHUNK, d).astype(BF16)
    zero_conv = jnp.zeros((bp, N_FF_CHUNKS, CONV_PAD, 2 * FF_CHUNK), F32)
    zero_pool = jnp.zeros((bp, HIST_ROWS, B_WIDTH), F32)

    dkp, dvp, plp, skp, svp, fcp = [], [], [], [], [], []
    dks, dvs, pls, sks, svs, fcs = [], [], [], [], [], []
    for i in range(depth):
        j = i // 2
        g_attn = norm_attn[i][None, :]
        if i % 2 == 0:
            lam_init = 0.8 - 0.6 * math.exp(-0.3 * i)
            w_in = w_in_even[j].astype(BF16)
            w_out = w_out_even[j].astype(BF16)
            pw = pool_w[j].astype(BF16)
            ps = pool_scale[j][None, :]
            sub = diff_subln[j][None, :]
            lam_p = diff_lambda[j]

            q, k, kz, v, vb, u = _even_in(hp, g_attn, w_in, tabs_p, ROW_TILE)
            a = _diff_prompt(q, kz, vb, lam_p, sub, bp, sp, lam_init)
            hp = _even_out(a, u, zero_pool, hp, pw, ps, w_out, bp, ROW_TILE, 0)
            dkp.append(k.reshape(bp, sp, A_HEADS, 2, A_DH))
            dvp.append(v.reshape(bp, sp, A_HEADS, 2 * A_DH))
            plp.append(u.reshape(bp, sp, B_WIDTH)[:, sp - POOL_HIST:])

            q, k, kz, v, vb, u = _even_in(hs, g_attn, w_in, tabs_s, tm_s)
            a = _diff_sample(q, kz, vb, cache_diff_k[j].reshape(bs, past, A_WIDTH),
                             cache_diff_v[j].reshape(bs, past, A_WIDTH), lam_p, sub, lam_init)
            hist = jnp.concatenate([jnp.zeros((bs, HIST_ROWS - POOL_HIST, B_WIDTH), F32), state_pool[j]], axis=1)
            hs = _even_out(a, u, hist, hs, pw, ps, w_out, bs, ss, POOL_HIST)
            dks.append(k.reshape(bs, ss, A_HEADS, 2, A_DH))
            dvs.append(v.reshape(bs, ss, A_HEADS, 2 * A_DH))
            pls.append(u.reshape(bs, ss, B_WIDTH)[:, ss - POOL_HIST:])
        else:
            w_in = w_in_odd[j].astype(BF16)
            b_in = b_in_odd[j][None, :]
            w_out = w_out_odd[j].astype(BF16)
            sk = sinks[j]

            q, k, v, kk, vv = _odd_in(hp, g_attn, w_in, b_in, tabs_p, ROW_TILE)
            o = _swa_prompt(sk, q, kk, vv, bp, sp)
            hp = _odd_out(o, hp, w_out, ROW_TILE)
            skp.append(k.reshape(bp, sp, C_KV, C_DH)[:, sp - C_CACHE:])
            svp.append(v.reshape(bp, sp, C_KV, C_DH)[:, sp - C_CACHE:])

            q, k, v, kk, vv = _odd_in(hs, g_attn, w_in, b_in, tabs_s, tm_s)
            o = _swa_sample(sk, q, kk, vv, cache_swa_k[j].reshape(bs, C_CACHE, C_KV * C_DH),
                            cache_swa_v[j].reshape(bs, C_CACHE, C_KV * C_DH))
            hs = _odd_out(o, hs, w_out, tm_s)
            k_all = jnp.concatenate([cache_swa_k[j], k.reshape(bs, ss, C_KV, C_DH)], axis=1)
            v_all = jnp.concatenate([cache_swa_v[j], v.reshape(bs, ss, C_KV, C_DH)], axis=1)
            sks.append(k_all[:, -C_CACHE:])
            svs.append(v_all[:, -C_CACHE:])

        last = i == depth - 1
        g_ffn = norm_ffn[i][None, :]
        g_fin = norm_final[None, :]
        hp, st = _ffn(hp, g_ffn, zero_conv, wu_c[i], cw_c[i], cb_c[i], wd_c[i], g_fin, bp, ROW_TILE, last)
        fcp.append(_conv_state(st))
        hs, st = _ffn(hs, g_ffn, _conv_hist(state_ffn_conv[i]), wu_c[i], cw_c[i], cb_c[i], wd_c[i], g_fin,
                      bs, ss, last)
        fcs.append(_conv_state(st))

    return (hp.reshape(bp, sp, d), hs.reshape(bs, ss, d),
            jnp.stack(dkp), jnp.stack(dvp), jnp.stack(plp), jnp.stack(skp), jnp.stack(svp), jnp.stack(fcp),
            jnp.stack(dks), jnp.stack(dvs), jnp.stack(pls), jnp.stack(sks), jnp.stack(svs), jnp.stack(fcs))
```

```python
import functools
import math

import jax
import jax.numpy as jnp
from jax import lax
from jax.experimental import pallas as pl
from jax.experimental.pallas import tpu as pltpu

F32 = jnp.float32
BF16 = jnp.bfloat16

D_MODEL = 1024
CHUNK = 64
ROPE_THETA = 10000.0
EPS = 1e-5
A_HEADS = 4
A_DH = 64
A_WIDTH = A_HEADS * 2 * A_DH
POOL_WINDOWS = (2, 4, 8, 16)
POOL_CH = 128
POOL_HIST = 15
B_WIDTH = 512
C_HEADS = 16
C_KV = 2
C_DH = 64
C_CACHE = 128
D_FF = 2816
CONV_W = 3

LOG2E = math.log2(math.e)
LANES = 128
SUBLANES = 8
HIST_ROWS = 16
FF_CHUNK = 256
N_FF_CHUNKS = D_FF // FF_CHUNK
ROW_TILE = 512
ATTN_TILE = 512
SWA_TILE = 2 * CHUNK
VMEM_LIMIT = 56 * 1024 * 1024

NT_DIMS = (((1,), (1,)), ((), ()))


def _params(n_axes, vmem=VMEM_LIMIT):
    return pltpu.CompilerParams(dimension_semantics=("arbitrary",) * n_axes, vmem_limit_bytes=vmem)


def _rmsnorm_rows(x, g):
    return x * lax.rsqrt(jnp.mean(x * x, axis=-1, keepdims=True) + EPS) * g


def _rope128(z, cos, s_up, s_dn):
    return z * cos + pltpu.roll(z, 96, 1) * s_up + pltpu.roll(z, 32, 1) * s_dn


def _rope_tables(pos):
    inv = ROPE_THETA ** (-jnp.arange(0, A_DH, 2, dtype=F32) / A_DH)
    ang = pos.astype(F32)[:, None] * inv[None, :]
    cos, sin = jnp.cos(ang), jnp.sin(ang)
    zero = jnp.zeros_like(sin)
    cos128 = jnp.tile(cos, (1, 4))
    s_up = jnp.tile(jnp.concatenate([-sin, zero], axis=1), (1, 2))
    s_dn = jnp.tile(jnp.concatenate([zero, sin], axis=1), (1, 2))
    return cos128, s_up, s_dn


def _even_in_kernel(x_ref, g_ref, w_ref, cos_ref, sup_ref, sdn_ref,
                    q_ref, k_ref, kz_ref, v_ref, vx_ref, u_ref, *, transpose_v):
    tm = x_ref.shape[0]
    hn = _rmsnorm_rows(x_ref[...], g_ref[...]).astype(BF16)
    z = jnp.dot(hn, w_ref[...], preferred_element_type=F32)
    cos, s_up, s_dn = cos_ref[...], sup_ref[...], sdn_ref[...]
    lo = lax.broadcasted_iota(jnp.int32, (tm, LANES), 1) < A_DH
    for c in range(A_HEADS):
        sl = slice(c * LANES, (c + 1) * LANES)
        rq = _rope128(z[:, sl], cos, s_up, s_dn)
        q_ref[:, sl] = (rq * (A_DH ** -0.5 * LOG2E)).astype(BF16)
        rk = _rope128(z[:, A_WIDTH + c * LANES:A_WIDTH + (c + 1) * LANES], cos, s_up, s_dn)
        k_ref[:, sl] = rk
        kz_ref[0, :, sl] = jnp.where(lo, rk, 0.0).astype(BF16)
        kz_ref[1, :, sl] = jnp.where(lo, 0.0, rk).astype(BF16)
    v = z[:, 2 * A_WIDTH:3 * A_WIDTH]
    v_ref[...] = v
    if transpose_v:
        vx_ref[0] = v.T.astype(BF16)
    else:
        vx_ref[...] = v.astype(BF16)
    u_ref[...] = z[:, 3 * A_WIDTH:]


def _even_in(x, g, w, tabs, tm, transpose_v):
    t = x.shape[0]
    n_tab = tabs[0].shape[0] // tm
    row = lambda i: (i, 0)
    tab = lambda i: (i % n_tab, 0)
    const = lambda i: (0, 0)
    if transpose_v:
        vx_spec = pl.BlockSpec((1, A_WIDTH, tm), lambda i: (i, 0, 0))
        vx_shape = jax.ShapeDtypeStruct((t // tm, A_WIDTH, tm), BF16)
    else:
        vx_spec = pl.BlockSpec((tm, A_WIDTH), row)
        vx_shape = jax.ShapeDtypeStruct((t, A_WIDTH), BF16)
    return pl.pallas_call(
        functools.partial(_even_in_kernel, transpose_v=transpose_v),
        grid=(t // tm,),
        in_specs=[pl.BlockSpec((tm, D_MODEL), row), pl.BlockSpec((1, D_MODEL), const),
                  pl.BlockSpec(w.shape, const),
                  pl.BlockSpec((tm, LANES), tab), pl.BlockSpec((tm, LANES), tab), pl.BlockSpec((tm, LANES), tab)],
        out_specs=[pl.BlockSpec((tm, A_WIDTH), row), pl.BlockSpec((tm, A_WIDTH), row),
                   pl.BlockSpec((2, tm, A_WIDTH), lambda i: (0, i, 0)),
                   pl.BlockSpec((tm, A_WIDTH), row), vx_spec,
                   pl.BlockSpec((tm, B_WIDTH), row)],
        out_shape=[jax.ShapeDtypeStruct((t, A_WIDTH), BF16), jax.ShapeDtypeStruct((t, A_WIDTH), F32),
                   jax.ShapeDtypeStruct((2, t, A_WIDTH), BF16),
                   jax.ShapeDtypeStruct((t, A_WIDTH), F32), vx_shape,
                   jax.ShapeDtypeStruct((t, B_WIDTH), F32)],
        compiler_params=_params(1), name="even_in",
    )(x, g, w, *tabs)


def _odd_in_kernel(x_ref, g_ref, w_ref, b_ref, cos_ref, sup_ref, sdn_ref,
                   q_ref, k_ref, v_ref, kk_ref, vv_ref, *, transpose_v):
    tm = x_ref.shape[0]
    hn = _rmsnorm_rows(x_ref[...], g_ref[...]).astype(BF16)
    z = jnp.dot(hn, w_ref[...], preferred_element_type=F32) + b_ref[...]
    cos, s_up, s_dn = cos_ref[...], sup_ref[...], sdn_ref[...]
    nq = C_HEADS * C_DH
    for c in range(nq // LANES):
        sl = slice(c * LANES, (c + 1) * LANES)
        q_ref[:, sl] = (_rope128(z[:, sl], cos, s_up, s_dn) * (C_DH ** -0.5 * LOG2E)).astype(BF16)
    k = _rope128(z[:, nq:nq + LANES], cos, s_up, s_dn)
    v = z[:, nq + LANES:nq + 2 * LANES]
    k_ref[...] = k
    v_ref[...] = v
    lo = lax.broadcasted_iota(jnp.int32, (tm, LANES), 1) < C_DH
    for src, dst, tr in ((k, kk_ref, False), (v, vv_ref, transpose_v)):
        h0 = jnp.where(lo, src, 0.0)
        h1 = jnp.where(lo, 0.0, src)
        for n, val in enumerate((h0, pltpu.roll(h0, C_DH, 1), pltpu.roll(h1, C_DH, 1), h1)):
            dst[n] = (val.T if tr else val).astype(BF16)


def _odd_in(x, g, w, b, tabs, tm, transpose_v):
    t = x.shape[0]
    n_tab = tabs[0].shape[0] // tm
    nq = C_HEADS * C_DH
    row = lambda i: (i, 0)
    tab = lambda i: (i % n_tab, 0)
    const = lambda i: (0, 0)
    if transpose_v:
        vv_spec = pl.BlockSpec((4, LANES, tm), lambda i: (0, 0, i))
        vv_shape = jax.ShapeDtypeStruct((4, LANES, t), BF16)
    else:
        vv_spec = pl.BlockSpec((4, tm, LANES), lambda i: (0, i, 0))
        vv_shape = jax.ShapeDtypeStruct((4, t, LANES), BF16)
    return pl.pallas_call(
        functools.partial(_odd_in_kernel, transpose_v=transpose_v),
        grid=(t // tm,),
        in_specs=[pl.BlockSpec((tm, D_MODEL), row), pl.BlockSpec((1, D_MODEL), const),
                  pl.BlockSpec(w.shape, const), pl.BlockSpec((1, w.shape[1]), const),
                  pl.BlockSpec((tm, LANES), tab), pl.BlockSpec((tm, LANES), tab), pl.BlockSpec((tm, LANES), tab)],
        out_specs=[pl.BlockSpec((tm, nq), row), pl.BlockSpec((tm, LANES), row), pl.BlockSpec((tm, LANES), row),
                   pl.BlockSpec((4, tm, LANES), lambda i: (0, i, 0)), vv_spec],
        out_shape=[jax.ShapeDtypeStruct((t, nq), BF16), jax.ShapeDtypeStruct((t, LANES), F32),
                   jax.ShapeDtypeStruct((t, LANES), F32),
                   jax.ShapeDtypeStruct((4, t, LANES), BF16), vv_shape],
        compiler_params=_params(1), name="odd_in",
    )(x, g, w, b, *tabs)


def _diff_lambda(lam_ref, lam_init):
    lp = lam_ref[...]
    return (jnp.exp(jnp.sum(lp[0:1] * lp[1:2], axis=-1, keepdims=True))
            - jnp.exp(jnp.sum(lp[2:3] * lp[3:4], axis=-1, keepdims=True)) + lam_init)


def _diff_prompt_kernel(q_ref, kz_ref, vt_ref, lam_ref, sub_ref, o_ref, m_ref, l_ref, acc_ref, *, lam_init):
    i = pl.program_id(2)
    tq = q_ref.shape[0]
    tk = tq
    q = q_ref[...]
    m_ref[...] = jnp.full(m_ref.shape, -jnp.inf, F32)
    l_ref[...] = jnp.zeros(l_ref.shape, F32)
    acc_ref[...] = jnp.zeros(acc_ref.shape, F32)

    def tile(j, mask):
        start = pl.multiple_of(j * tk, tk)
        vt = vt_ref[j]
        for mp in range(2):
            s = lax.dot_general(kz_ref[mp, pl.ds(start, tk), :], q, NT_DIMS,
                                preferred_element_type=F32)
            if mask is not None:
                s = jnp.where(mask, s, -jnp.inf)
            m_prev = m_ref[mp]
            m_new = jnp.maximum(m_prev, jnp.max(s, axis=0, keepdims=True))
            alpha = jnp.exp2(m_prev - m_new)
            p = jnp.exp2(s - m_new)
            l_ref[mp] = alpha * l_ref[mp] + jnp.sum(p, axis=0, keepdims=True)
            acc_ref[mp] = alpha * acc_ref[mp] + jnp.dot(vt, p.astype(BF16), preferred_element_type=F32)
            m_ref[mp] = m_new

    def body(j, carry):
        tile(j, None)
        return carry

    lax.fori_loop(0, i, body, 0)
    k_chunk = lax.broadcasted_iota(jnp.int32, (tk, tq), 0) // CHUNK
    q_chunk = lax.broadcasted_iota(jnp.int32, (tk, tq), 1) // CHUNK
    tile(i, k_chunk <= q_chunk)
    lam = _diff_lambda(lam_ref, lam_init)
    a = acc_ref[0] / l_ref[0] - lam * (acc_ref[1] / l_ref[1])
    a = a * lax.rsqrt(jnp.mean(a * a, axis=0, keepdims=True) + EPS) * sub_ref[...] * (1.0 - lam_init)
    o_ref[...] = a.T.astype(BF16)


def _diff_prompt(q, kz, vt, lam_p, sub_col, batch, seq, lam_init):
    t = q.shape[0]
    tq = ATTN_TILE
    nq = seq // tq
    assert vt.shape == (t // tq, A_WIDTH, tq)
    return pl.pallas_call(
        functools.partial(_diff_prompt_kernel, lam_init=lam_init),
        grid=(batch, A_HEADS, nq),
        in_specs=[pl.BlockSpec((tq, LANES), lambda b, h, i: (b * nq + i, h)),
                  pl.BlockSpec((2, seq, LANES), lambda b, h, i: (0, b, h)),
                  pl.BlockSpec((nq, LANES, tq), lambda b, h, i: (b, h, 0)),
                  pl.BlockSpec((4, A_DH), lambda b, h, i: (0, 0)),
                  pl.BlockSpec((LANES, 1), lambda b, h, i: (0, 0))],
        out_specs=pl.BlockSpec((tq, LANES), lambda b, h, i: (b * nq + i, h)),
        out_shape=jax.ShapeDtypeStruct((t, A_WIDTH), BF16),
        scratch_shapes=[pltpu.VMEM((2, 1, tq), F32), pltpu.VMEM((2, 1, tq), F32),
                        pltpu.VMEM((2, LANES, tq), F32)],
        compiler_params=_params(3), name="diff_attn_prompt",
    )(q, kz, vt, lam_p, sub_col)


def _diff_sample_kernel(q_ref, kz_ref, v_ref, ck_ref, cv_ref, lam_ref, sub_ref, o_ref, *, lam_init):
    past = ck_ref.shape[1]
    lo = lax.broadcasted_iota(jnp.int32, (past, LANES), 1) < A_DH
    lam = _diff_lambda(lam_ref, lam_init)
    for h in range(A_HEADS):
        sl = slice(h * LANES, (h + 1) * LANES)
        q = q_ref[:, sl]
        ck = ck_ref[0, :, sl]
        cv = cv_ref[0, :, sl].astype(BF16)
        vn = v_ref[:, sl]
        outs = []
        for mp in range(2):
            ckm = (jnp.where(lo, ck, 0.0) if mp == 0 else jnp.where(lo, 0.0, ck)).astype(BF16)
            s_c = lax.dot_general(q, ckm, NT_DIMS, preferred_element_type=F32)
            s_n = lax.dot_general(q, kz_ref[mp, :, sl], NT_DIMS, preferred_element_type=F32)
            m = jnp.maximum(jnp.max(s_c, axis=-1, keepdims=True), jnp.max(s_n, axis=-1, keepdims=True))
            e_c = jnp.exp2(s_c - m)
            e_n = jnp.exp2(s_n - m)
            den = jnp.sum(e_c, axis=-1, keepdims=True) + jnp.sum(e_n, axis=-1, keepdims=True)
            o = (jnp.dot(e_c.astype(BF16), cv, preferred_element_type=F32)
                 + jnp.dot(e_n.astype(BF16), vn, preferred_element_type=F32))
            outs.append(o / den)
        a = outs[0] - lam * outs[1]
        o_ref[:, sl] = (_rmsnorm_rows(a, sub_ref[...]) * (1.0 - lam_init)).astype(BF16)


def _diff_sample(q, kz, vb, cache_k, cache_v, lam_p, sub, lam_init):
    nb, past, _ = cache_k.shape
    t = q.shape[0]
    sq = t // nb
    return pl.pallas_call(
        functools.partial(_diff_sample_kernel, lam_init=lam_init),
        grid=(nb,),
        in_specs=[pl.BlockSpec((sq, A_WIDTH), lambda b: (b, 0)),
                  pl.BlockSpec((2, sq, A_WIDTH), lambda b: (0, b, 0)),
                  pl.BlockSpec((sq, A_WIDTH), lambda b: (b, 0)),
                  pl.BlockSpec((1, past, A_WIDTH), lambda b: (b, 0, 0)),
                  pl.BlockSpec((1, past, A_WIDTH), lambda b: (b, 0, 0)),
                  pl.BlockSpec((4, A_DH), lambda b: (0, 0)),
                  pl.BlockSpec((1, LANES), lambda b: (0, 0))],
        out_specs=pl.BlockSpec((sq, A_WIDTH), lambda b: (b, 0)),
        out_shape=jax.ShapeDtypeStruct((t, A_WIDTH), BF16),
        compiler_params=_params(1), name="diff_attn_sample",
    )(q, kz, vb, cache_k, cache_v, lam_p, sub)


def _even_out_kernel(a_ref, u_ref, hist_ref, x_ref, pw_ref, ps_ref, w_ref, o_ref, ext_ref, *, n_hist):
    i = pl.program_id(1)
    tm = u_ref.shape[0]

    @pl.when(i == 0)
    def _():
        ext_ref[0:HIST_ROWS, :] = hist_ref[0]

    u = u_ref[...]
    ext_ref[HIST_ROWS:HIST_ROWS + tm, :] = u
    pos = (i * tm + lax.broadcasted_iota(jnp.int32, (tm, 1), 0)).astype(F32)
    acc = x_ref[...] + jnp.dot(a_ref[...], w_ref[0:A_WIDTH, :], preferred_element_type=F32)
    for g, win in enumerate(POOL_WINDOWS):
        sl = slice(g * POOL_CH, (g + 1) * POOL_CH)
        tot = u[:, sl]
        for back in range(1, win):
            tot = tot + ext_ref[HIST_ROWS - back:HIST_ROWS - back + tm, sl]
        cnt = jnp.minimum(float(win), pos + (1.0 + n_hist))
        pooled = (tot / cnt - u[:, sl]).astype(BF16)
        y = jnp.dot(pooled, pw_ref[g], preferred_element_type=F32) * ps_ref[:, sl]
        acc = acc + jnp.dot(y.astype(BF16), w_ref[A_WIDTH + g * POOL_CH:A_WIDTH + (g + 1) * POOL_CH, :],
                            preferred_element_type=F32)
    o_ref[...] = acc
    ext_ref[0:HIST_ROWS, :] = ext_ref[tm:tm + HIST_ROWS, :]


def _even_out(a, u, hist, x, pool_w, pool_scale, w_out, nseq, tm, n_hist):
    t = x.shape[0]
    nt = t // nseq // tm
    row = lambda b, i: (b * nt + i, 0)
    return pl.pallas_call(
        functools.partial(_even_out_kernel, n_hist=n_hist),
        grid=(nseq, nt),
        in_specs=[pl.BlockSpec((tm, A_WIDTH), row), pl.BlockSpec((tm, B_WIDTH), row),
                  pl.BlockSpec((1, HIST_ROWS, B_WIDTH), lambda b, i: (b, 0, 0)),
                  pl.BlockSpec((tm, D_MODEL), row),
                  pl.BlockSpec(pool_w.shape, lambda b, i: (0, 0, 0)),
                  pl.BlockSpec((1, B_WIDTH), lambda b, i: (0, 0)),
                  pl.BlockSpec(w_out.shape, lambda b, i: (0, 0))],
        out_specs=pl.BlockSpec((tm, D_MODEL), row),
        out_shape=jax.ShapeDtypeStruct((t, D_MODEL), F32),
        scratch_shapes=[pltpu.VMEM((HIST_ROWS + tm, B_WIDTH), F32)],
        compiler_params=_params(2), name="even_out",
    )(a, u, hist, x, pool_w, pool_scale, w_out)


def _swa_prompt_kernel(sink_ref, q_ref, kp_ref, kc_ref, vp_ref, vc_ref, o_ref):
    i = pl.program_id(1)
    tq = q_ref.shape[0]
    nk = 2 * tq
    nq = 4 * tq
    kc = lax.broadcasted_iota(jnp.int32, (2 * nk, nq), 0) % nk // CHUNK
    qc = lax.broadcasted_iota(jnp.int32, (2 * nk, nq), 1) % tq // CHUNK
    vis = (kc >= qc) & (kc <= qc + 2) & ((kc >= 2) | (i > 0))
    pair = lax.broadcasted_iota(jnp.int32, (1, nq), 1) // tq
    low_rows = lax.broadcasted_iota(jnp.int32, (LANES, 1), 0) < C_DH
    for kv in range(C_KV):
        qs = jnp.concatenate([q_ref[:, (kv * 4 + p) * LANES:(kv * 4 + p + 1) * LANES] for p in range(4)], axis=0)
        ks = jnp.concatenate([kp_ref[2 * kv], kc_ref[2 * kv], kp_ref[2 * kv + 1], kc_ref[2 * kv + 1]], axis=0)
        vt = jnp.concatenate([vp_ref[2 * kv], vc_ref[2 * kv], vp_ref[2 * kv + 1], vc_ref[2 * kv + 1]], axis=1)
        s = lax.dot_general(ks, qs, NT_DIMS, preferred_element_type=F32)
        s = jnp.where(vis, s, -jnp.inf)
        es, rdens = [], []
        for half in range(2):
            sink = jnp.zeros((1, nq), F32)
            for p in range(4):
                sink = jnp.where(pair == p, sink_ref[kv * 8 + 2 * p + half] * LOG2E, sink)
            sh = s[half * nk:(half + 1) * nk]
            m = jnp.maximum(jnp.max(sh, axis=0, keepdims=True), sink)
            e = jnp.exp2(sh - m)
            rdens.append(1.0 / (jnp.sum(e, axis=0, keepdims=True) + jnp.exp2(sink - m)))
            es.append(e.astype(BF16))
        o = jnp.dot(vt, jnp.concatenate(es, axis=0), preferred_element_type=F32)
        o = (o * jnp.where(low_rows, rdens[0], rdens[1])).T
        for pr in range(4):
            o_ref[:, (kv * 4 + pr) * LANES:(kv * 4 + pr + 1) * LANES] = o[pr * tq:(pr + 1) * tq].astype(BF16)


def _swa_prompt(sinks, q, kk, vvt, batch, seq):
    t = q.shape[0]
    tq = SWA_TILE
    nt = seq // tq
    kspec = lambda f: pl.BlockSpec((4, tq, LANES), f)
    vspec = lambda f: pl.BlockSpec((4, LANES, tq), f)
    return pl.pallas_call(
        _swa_prompt_kernel,
        grid=(batch, nt),
        in_specs=[pl.BlockSpec(memory_space=pltpu.SMEM),
                  pl.BlockSpec((tq, C_HEADS * C_DH), lambda b, i: (b * nt + i, 0)),
                  kspec(lambda b, i: (0, b * nt + jnp.maximum(i - 1, 0), 0)), kspec(lambda b, i: (0, b * nt + i, 0)),
                  vspec(lambda b, i: (0, 0, b * nt + jnp.maximum(i - 1, 0))), vspec(lambda b, i: (0, 0, b * nt + i))],
        out_specs=pl.BlockSpec((tq, C_HEADS * C_DH), lambda b, i: (b * nt + i, 0)),
        out_shape=jax.ShapeDtypeStruct((t, C_HEADS * C_DH), BF16),
        compiler_params=_params(2), name="swa_prompt",
    )(sinks, q, kk, kk, vvt, vvt)


def _swa_sample_kernel(sink_ref, q_ref, kk_ref, vv_ref, ck_ref, cv_ref, o_ref):
    sq = q_ref.shape[0]
    nc = ck_ref.shape[1]
    lo = lax.broadcasted_iota(jnp.int32, (nc, LANES), 1) < C_DH

    def halves(c, kv):
        own = jnp.where(lo, c, 0.0) if kv == 0 else jnp.where(lo, 0.0, c)
        swapped = pltpu.roll(own, C_DH, 1)
        return ((own, swapped) if kv == 0 else (swapped, own))

    ck = ck_ref[0]
    cv = cv_ref[0]
    for kv in range(C_KV):
        qs = jnp.concatenate([q_ref[:, (kv * 4 + p) * LANES:(kv * 4 + p + 1) * LANES] for p in range(4)], axis=0)
        ckh = halves(ck, kv)
        cvh = halves(cv, kv)
        o = jnp.zeros((4 * sq, LANES), F32)
        for half in range(2):
            ks = jnp.concatenate([ckh[half].astype(BF16), kk_ref[2 * kv + half]], axis=0)
            vs = jnp.concatenate([cvh[half].astype(BF16), vv_ref[2 * kv + half]], axis=0)
            s = lax.dot_general(qs, ks, NT_DIMS, preferred_element_type=F32)
            sink = jnp.concatenate(
                [jnp.full((sq, 1), sink_ref[kv * 8 + 2 * p + half] * LOG2E, F32) for p in range(4)], axis=0)
            m = jnp.maximum(jnp.max(s, axis=-1, keepdims=True), sink)
            e = jnp.exp2(s - m)
            p = e / (jnp.sum(e, axis=-1, keepdims=True) + jnp.exp2(sink - m))
            o = o + jnp.dot(p.astype(BF16), vs, preferred_element_type=F32)
        for pr in range(4):
            o_ref[:, (kv * 4 + pr) * LANES:(kv * 4 + pr + 1) * LANES] = o[pr * sq:(pr + 1) * sq].astype(BF16)


def _swa_sample(sinks, q, kk, vv, cache_k, cache_v):
    nb, nc, _ = cache_k.shape
    t = q.shape[0]
    sq = t // nb
    return pl.pallas_call(
        _swa_sample_kernel,
        grid=(nb,),
        in_specs=[pl.BlockSpec(memory_space=pltpu.SMEM),
                  pl.BlockSpec((sq, C_HEADS * C_DH), lambda b: (b, 0)),
                  pl.BlockSpec((4, sq, LANES), lambda b: (0, b, 0)),
                  pl.BlockSpec((4, sq, LANES), lambda b: (0, b, 0)),
                  pl.BlockSpec((1, nc, LANES), lambda b: (b, 0, 0)),
                  pl.BlockSpec((1, nc, LANES), lambda b: (b, 0, 0))],
        out_specs=pl.BlockSpec((sq, C_HEADS * C_DH), lambda b: (b, 0)),
        out_shape=jax.ShapeDtypeStruct((t, C_HEADS * C_DH), BF16),
        compiler_params=_params(1), name="swa_sample",
    )(sinks, q, kk, vv, cache_k, cache_v)


def _odd_out_kernel(o_ref, x_ref, w_ref, y_ref):
    y_ref[...] = x_ref[...] + jnp.dot(o_ref[...], w_ref[...], preferred_element_type=F32)


def _odd_out(o, x, w, tm):
    t = x.shape[0]
    row = lambda i: (i, 0)
    return pl.pallas_call(
        _odd_out_kernel,
        grid=(t // tm,),
        in_specs=[pl.BlockSpec((tm, o.shape[1]), row), pl.BlockSpec((tm, D_MODEL), row),
                  pl.BlockSpec(w.shape, lambda i: (0, 0))],
        out_specs=pl.BlockSpec((tm, D_MODEL), row),
        out_shape=jax.ShapeDtypeStruct((t, D_MODEL), F32),
        compiler_params=_params(1), name="odd_out",
    )(o, x, w)


def _shift_rows(up, prev, shift, nseq):
    seq_len = up.shape[0] // nseq
    rolled = pltpu.roll(up, shift, 0)
    first = lax.broadcasted_iota(jnp.int32, (SUBLANES, up.shape[1]), 0) < shift
    pieces = []
    for s in range(nseq):
        lo = s * seq_len
        head = jnp.where(first, pltpu.roll(prev[s], shift, 0), pltpu.roll(up[lo:lo + SUBLANES], shift, 0))
        pieces += [head, rolled[lo + SUBLANES:lo + seq_len]]
    return jnp.concatenate(pieces, axis=0)


def _ffn_kernel(x_ref, g_ref, hist_ref, wu_ref, cw_ref, cb_ref, wd_ref, gf_ref,
                y_ref, st_ref, hn_ref, h_ref, carry_ref, *, nseq, final_norm):
    i = pl.program_id(1)
    tm = x_ref.shape[0]
    seq_len = tm // nseq

    @pl.when(i == 0)
    def _():
        carry_ref[...] = hist_ref[0]

    x = x_ref[...]
    hn_ref[...] = _rmsnorm_rows(x, g_ref[...]).astype(BF16)
    for c in range(N_FF_CHUNKS):
        up = jnp.dot(hn_ref[...], wu_ref[c], preferred_element_type=F32)
        prev = [carry_ref[s, c] for s in range(nseq)]
        cw = cw_ref[c]
        conv = (cb_ref[c] + _shift_rows(up, prev, 2, nseq) * cw[0:1]
                + _shift_rows(up, prev, 1, nseq) * cw[1:2] + up * cw[2:3])
        for s in range(nseq):
            carry_ref[s, c] = up[(s + 1) * seq_len - SUBLANES:(s + 1) * seq_len]
        gate = conv[:, :FF_CHUNK]
        h_ref[:, c * FF_CHUNK:(c + 1) * FF_CHUNK] = (gate * jax.nn.sigmoid(gate) * conv[:, FF_CHUNK:]).astype(BF16)
    y = x + jnp.dot(h_ref[...], wd_ref[...], preferred_element_type=F32)
    if final_norm:
        y = _rmsnorm_rows(y, gf_ref[...])
    y_ref[...] = y
    st_ref[0] = carry_ref[...]


def _ffn(x, g, hist, wu, cw, cb, wd, g_final, nseq, tm, final_norm):
    t = x.shape[0]
    n_outer = hist.shape[0]
    nt = t // n_outer // tm
    row = lambda b, i: (b * nt + i, 0)
    c2 = lambda b, i: (0, 0)
    c3 = lambda b, i: (0, 0, 0)
    st_block = (1,) + hist.shape[1:]
    st_map = lambda b, i: (b, 0, 0, 0, 0)
    return pl.pallas_call(
        functools.partial(_ffn_kernel, nseq=nseq, final_norm=final_norm),
        grid=(n_outer, nt),
        in_specs=[pl.BlockSpec((tm, D_MODEL), row), pl.BlockSpec((1, D_MODEL), c2),
                  pl.BlockSpec(st_block, st_map),
                  pl.BlockSpec(wu.shape, c3), pl.BlockSpec(cw.shape, c3), pl.BlockSpec(cb.shape, c3),
                  pl.BlockSpec(wd.shape, c2), pl.BlockSpec((1, D_MODEL), c2)],
        out_specs=[pl.BlockSpec((tm, D_MODEL), row), pl.BlockSpec(st_block, st_map)],
        out_shape=[jax.ShapeDtypeStruct((t, D_MODEL), F32), jax.ShapeDtypeStruct(hist.shape, F32)],
        scratch_shapes=[pltpu.VMEM((tm, D_MODEL), BF16),
                        pltpu.VMEM((tm, D_FF), BF16),
                        pltpu.VMEM(hist.shape[1:], F32)],
        compiler_params=_params(2), name="conv_ffn",
    )(x, g, hist, wu, cw, cb, wd, g_final)


def _chunk_cols(a):
    lead = a.shape[:-1]
    a = a.reshape(lead + (2, N_FF_CHUNKS, FF_CHUNK))
    a = jnp.moveaxis(a, -3, -2)
    return a.reshape(lead + (N_FF_CHUNKS, 2 * FF_CHUNK))


def _unchunk_cols(a):
    lead = a.shape[:-2]
    a = a.reshape(lead + (N_FF_CHUNKS, 2, FF_CHUNK))
    a = jnp.moveaxis(a, -2, -3)
    return a.reshape(lead + (2 * D_FF,))


def _conv_hist(state):
    b = state.shape[0]
    s = jnp.moveaxis(_chunk_cols(state), 1, 2)
    pad = jnp.zeros((b, N_FF_CHUNKS, SUBLANES - (CONV_W - 1), 2 * FF_CHUNK), F32)
    return jnp.concatenate([pad, s], axis=2)


def _conv_state(st):
    s = st[:, :, SUBLANES - (CONV_W - 1):, :]
    return _unchunk_cols(jnp.moveaxis(s, 2, 1))


def kernel(x_prompt, x_sample, cache_diff_k, cache_diff_v, state_pool, cache_swa_k, cache_swa_v, state_ffn_conv,
           norm_attn, norm_ffn, norm_final, w_in_even, w_out_even, diff_lambda, diff_subln, pool_w, pool_scale,
           w_in_odd, b_in_odd, w_out_odd, sinks, w_up, conv_w, conv_b, w_down):
    bp, sp, d = x_prompt.shape
    bs, ss, _ = x_sample.shape
    depth = norm_attn.shape[0]
    past = cache_diff_k.shape[2]
    hp = x_prompt.reshape(bp * sp, d)
    hs = x_sample.reshape(bs * ss, d)
    tm_s = bs * ss

    tabs_p = _rope_tables(jnp.arange(sp, dtype=jnp.int32))
    tabs_s = tuple(jnp.tile(t, (bs, 1)) for t in _rope_tables(past + jnp.arange(ss, dtype=jnp.int32)))

    wu_c = jnp.moveaxis(_chunk_cols(w_up), 2, 1).astype(BF16)
    cw_c = jnp.moveaxis(_chunk_cols(conv_w), 2, 1)
    cb_c = _chunk_cols(conv_b)[:, :, None, :]
    wd_b = w_down.astype(BF16)
    zero_conv = jnp.zeros((bp, 1, N_FF_CHUNKS, SUBLANES, 2 * FF_CHUNK), F32)
    zero_pool = jnp.zeros((bp, HIST_ROWS, B_WIDTH), F32)

    dkp, dvp, plp, skp, svp, fcp = [], [], [], [], [], []
    dks, dvs, pls, sks, svs, fcs = [], [], [], [], [], []
    for i in range(depth):
        j = i // 2
        g_attn = norm_attn[i][None, :]
        if i % 2 == 0:
            lam_init = 0.8 - 0.6 * math.exp(-0.3 * i)
            w_in = w_in_even[j].astype(BF16)
            w_out = w_out_even[j].astype(BF16)
            pw = pool_w[j].astype(BF16)
            ps = pool_scale[j][None, :]
            lam_p = diff_lambda[j]

            q, k, kz, v, vt, u = _even_in(hp, g_attn, w_in, tabs_p, ROW_TILE, True)
            a = _diff_prompt(q, kz, vt, lam_p, diff_subln[j][:, None], bp, sp, lam_init)
            hp = _even_out(a, u, zero_pool, hp, pw, ps, w_out, bp, ROW_TILE, 0)
            dkp.append(k.reshape(bp, sp, A_HEADS, 2, A_DH))
            dvp.append(v.reshape(bp, sp, A_HEADS, 2 * A_DH))
            plp.append(u.reshape(bp, sp, B_WIDTH)[:, sp - POOL_HIST:])

            q, k, kz, v, vb, u = _even_in(hs, g_attn, w_in, tabs_s, tm_s, False)
            a = _diff_sample(q, kz, vb, cache_diff_k[j].reshape(bs, past, A_WIDTH),
                             cache_diff_v[j].reshape(bs, past, A_WIDTH), lam_p, diff_subln[j][None, :], lam_init)
            hist = jnp.concatenate([jnp.zeros((bs, HIST_ROWS - POOL_HIST, B_WIDTH), F32), state_pool[j]], axis=1)
            hs = _even_out(a, u, hist, hs, pw, ps, w_out, bs, ss, POOL_HIST)
            dks.append(k.reshape(bs, ss, A_HEADS, 2, A_DH))
            dvs.append(v.reshape(bs, ss, A_HEADS, 2 * A_DH))
            pls.append(u.reshape(bs, ss, B_WIDTH)[:, ss - POOL_HIST:])
        else:
            w_in = w_in_odd[j].astype(BF16)
            b_in = b_in_odd[j][None, :]
            w_out = w_out_odd[j].astype(BF16)
            sk = sinks[j]

            q, k, v, kk, vvt = _odd_in(hp, g_attn, w_in, b_in, tabs_p, ROW_TILE, True)
            o = _swa_prompt(sk, q, kk, vvt, bp, sp)
            hp = _odd_out(o, hp, w_out, ROW_TILE)
            skp.append(k.reshape(bp, sp, C_KV, C_DH)[:, sp - C_CACHE:])
            svp.append(v.reshape(bp, sp, C_KV, C_DH)[:, sp - C_CACHE:])

            q, k, v, kk, vv = _odd_in(hs, g_attn, w_in, b_in, tabs_s, tm_s, False)
            o = _swa_sample(sk, q, kk, vv, cache_swa_k[j].reshape(bs, C_CACHE, C_KV * C_DH),
                            cache_swa_v[j].reshape(bs, C_CACHE, C_KV * C_DH))
            hs = _odd_out(o, hs, w_out, tm_s)
            k_all = jnp.concatenate([cache_swa_k[j], k.reshape(bs, ss, C_KV, C_DH)], axis=1)
            v_all = jnp.concatenate([cache_swa_v[j], v.reshape(bs, ss, C_KV, C_DH)], axis=1)
            sks.append(k_all[:, -C_CACHE:])
            svs.append(v_all[:, -C_CACHE:])

        last = i == depth - 1
        g_ffn = norm_ffn[i][None, :]
        g_fin = norm_final[None, :]
        hp, st = _ffn(hp, g_ffn, zero_conv, wu_c[i], cw_c[i], cb_c[i], wd_b[i], g_fin, 1, ROW_TILE, last)
        fcp.append(_conv_state(st[:, 0]))
        hs, st = _ffn(hs, g_ffn, _conv_hist(state_ffn_conv[i])[None], wu_c[i], cw_c[i], cb_c[i], wd_b[i], g_fin,
                      bs, tm_s, last)
        fcs.append(_conv_state(st[0]))

    return (hp.reshape(bp, sp, d), hs.reshape(bs, ss, d),
            jnp.stack(dkp), jnp.stack(dvp), jnp.stack(plp), jnp.stack(skp), jnp.stack(svp), jnp.stack(fcp),
            jnp.stack(dks), jnp.stack(dvs), jnp.stack(pls), jnp.stack(sks), jnp.stack(svs), jnp.stack(fcs))
```

```python
import functools
import math

import jax
import jax.numpy as jnp
from jax import lax
from jax.experimental import pallas as pl
from jax.experimental.pallas import tpu as pltpu

F32 = jnp.float32
BF16 = jnp.bfloat16

D_MODEL = 1024
CHUNK = 64
ROPE_THETA = 10000.0
EPS = 1e-5
A_HEADS = 4
A_DH = 64
A_WIDTH = A_HEADS * 2 * A_DH
POOL_WINDOWS = (2, 4, 8, 16)
POOL_CH = 128
POOL_HIST = 15
B_WIDTH = 512
C_HEADS = 16
C_KV = 2
C_DH = 64
C_CACHE = 128
D_FF = 2816
CONV_W = 3

LOG2E = math.log2(math.e)
LANES = 128
SUBLANES = 8
HIST_ROWS = 16
FF_CHUNK = 256
N_FF_CHUNKS = D_FF // FF_CHUNK
ROW_TILE = 512
ATTN_TILE = 512
SWA_TILE = 2 * CHUNK
VMEM_LIMIT = 56 * 1024 * 1024

NT_DIMS = (((1,), (1,)), ((), ()))


def _params(n_axes, vmem=VMEM_LIMIT):
    return pltpu.CompilerParams(dimension_semantics=("arbitrary",) * n_axes, vmem_limit_bytes=vmem)


def _rmsnorm_rows(x, g):
    return x * lax.rsqrt(jnp.mean(x * x, axis=-1, keepdims=True) + EPS) * g


def _rope128(z, cos, s_up, s_dn):
    return z * cos + pltpu.roll(z, 96, 1) * s_up + pltpu.roll(z, 32, 1) * s_dn


def _rope_tables(pos):
    inv = ROPE_THETA ** (-jnp.arange(0, A_DH, 2, dtype=F32) / A_DH)
    ang = pos.astype(F32)[:, None] * inv[None, :]
    cos, sin = jnp.cos(ang), jnp.sin(ang)
    zero = jnp.zeros_like(sin)
    cos128 = jnp.tile(cos, (1, 4))
    s_up = jnp.tile(jnp.concatenate([-sin, zero], axis=1), (1, 2))
    s_dn = jnp.tile(jnp.concatenate([zero, sin], axis=1), (1, 2))
    return cos128, s_up, s_dn


def _even_in_kernel(x_ref, g_ref, w_ref, cos_ref, sup_ref, sdn_ref,
                    q_ref, k_ref, kz_ref, v_ref, vx_ref, u_ref, *, prompt):
    tm = x_ref.shape[0]
    hn = _rmsnorm_rows(x_ref[...], g_ref[...]).astype(BF16)
    z = jnp.dot(hn, w_ref[...], preferred_element_type=F32)
    cos, s_up, s_dn = cos_ref[...], sup_ref[...], sdn_ref[...]
    lo = lax.broadcasted_iota(jnp.int32, (tm, LANES), 1) < A_DH
    for c in range(A_HEADS):
        sl = slice(c * LANES, (c + 1) * LANES)
        rq = _rope128(z[:, sl], cos, s_up, s_dn)
        q_ref[:, sl] = (rq * (A_DH ** -0.5 * LOG2E)).astype(BF16)
        rk = _rope128(z[:, A_WIDTH + c * LANES:A_WIDTH + (c + 1) * LANES], cos, s_up, s_dn)
        if prompt:
            k_ref[0, sl, :] = rk.T
            kz_ref[0, :, sl] = jnp.where(lo, rk, 0.0).astype(BF16)
            kz_ref[1, :, sl] = jnp.where(lo, 0.0, rk).astype(BF16)
        else:
            k_ref[:, sl] = rk
            kz_ref[:, sl] = rk.astype(BF16)
    v = z[:, 2 * A_WIDTH:3 * A_WIDTH]
    v_ref[...] = v
    if prompt:
        vx_ref[0] = v.T.astype(BF16)
    else:
        vx_ref[...] = v.astype(BF16)
    u_ref[...] = z[:, 3 * A_WIDTH:]


def _even_in(x, g, w, tabs, tm, nseq, prompt):
    t = x.shape[0]
    n_tab = tabs[0].shape[0] // tm
    nt = t // nseq // tm
    row = lambda i: (i, 0)
    tab = lambda i: (i % n_tab, 0)
    const = lambda i: (0, 0)
    if prompt:
        k_spec = pl.BlockSpec((1, A_WIDTH, tm), lambda i: (i // nt, 0, i % nt))
        k_shape = jax.ShapeDtypeStruct((nseq, A_WIDTH, t // nseq), F32)
        kz_spec = pl.BlockSpec((2, tm, A_WIDTH), lambda i: (0, i, 0))
        kz_shape = jax.ShapeDtypeStruct((2, t, A_WIDTH), BF16)
        vx_spec = pl.BlockSpec((1, A_WIDTH, tm), lambda i: (i, 0, 0))
        vx_shape = jax.ShapeDtypeStruct((t // tm, A_WIDTH, tm), BF16)
    else:
        k_spec = pl.BlockSpec((tm, A_WIDTH), row)
        k_shape = jax.ShapeDtypeStruct((t, A_WIDTH), F32)
        kz_spec = pl.BlockSpec((tm, A_WIDTH), row)
        kz_shape = jax.ShapeDtypeStruct((t, A_WIDTH), BF16)
        vx_spec = pl.BlockSpec((tm, A_WIDTH), row)
        vx_shape = jax.ShapeDtypeStruct((t, A_WIDTH), BF16)
    return pl.pallas_call(
        functools.partial(_even_in_kernel, prompt=prompt),
        grid=(t // tm,),
        in_specs=[pl.BlockSpec((tm, D_MODEL), row), pl.BlockSpec((1, D_MODEL), const),
                  pl.BlockSpec(w.shape, const),
                  pl.BlockSpec((tm, LANES), tab), pl.BlockSpec((tm, LANES), tab), pl.BlockSpec((tm, LANES), tab)],
        out_specs=[pl.BlockSpec((tm, A_WIDTH), row), k_spec, kz_spec,
                   pl.BlockSpec((tm, A_WIDTH), row), vx_spec,
                   pl.BlockSpec((tm, B_WIDTH), row)],
        out_shape=[jax.ShapeDtypeStruct((t, A_WIDTH), BF16), k_shape, kz_shape,
                   jax.ShapeDtypeStruct((t, A_WIDTH), F32), vx_shape,
                   jax.ShapeDtypeStruct((t, B_WIDTH), F32)],
        compiler_params=_params(1), name="even_in",
    )(x, g, w, *tabs)


def _odd_in_kernel(x_ref, g_ref, w_ref, b_ref, cos_ref, sup_ref, sdn_ref,
                   q_ref, k_ref, v_ref, kk_ref, vv_ref, *, transpose_v):
    tm = x_ref.shape[0]
    hn = _rmsnorm_rows(x_ref[...], g_ref[...]).astype(BF16)
    z = jnp.dot(hn, w_ref[...], preferred_element_type=F32) + b_ref[...]
    cos, s_up, s_dn = cos_ref[...], sup_ref[...], sdn_ref[...]
    nq = C_HEADS * C_DH
    for c in range(nq // LANES):
        sl = slice(c * LANES, (c + 1) * LANES)
        q_ref[:, sl] = (_rope128(z[:, sl], cos, s_up, s_dn) * (C_DH ** -0.5 * LOG2E)).astype(BF16)
    k = _rope128(z[:, nq:nq + LANES], cos, s_up, s_dn)
    v = z[:, nq + LANES:nq + 2 * LANES]
    tail = k_ref.shape[0]
    k_ref[...] = k[tm - tail:]
    v_ref[...] = v[tm - tail:]
    lo = lax.broadcasted_iota(jnp.int32, (tm, LANES), 1) < C_DH
    for src, dst, tr in ((k, kk_ref, False), (v, vv_ref, transpose_v)):
        h0 = jnp.where(lo, src, 0.0)
        h1 = jnp.where(lo, 0.0, src)
        for n, val in enumerate((h0, pltpu.roll(h0, C_DH, 1), pltpu.roll(h1, C_DH, 1), h1)):
            dst[n] = (val.T if tr else val).astype(BF16)


def _odd_in(x, g, w, b, tabs, tm, nseq, tail, transpose_v):
    t = x.shape[0]
    n_tab = tabs[0].shape[0] // tm
    nt = t // nseq // tm
    nq = C_HEADS * C_DH
    row = lambda i: (i, 0)
    kv_spec = pl.BlockSpec((tail, LANES), lambda i: (i // nt, 0))
    kv_shape = jax.ShapeDtypeStruct((nseq * tail, LANES), F32)
    tab = lambda i: (i % n_tab, 0)
    const = lambda i: (0, 0)
    if transpose_v:
        vv_spec = pl.BlockSpec((4, LANES, tm), lambda i: (0, 0, i))
        vv_shape = jax.ShapeDtypeStruct((4, LANES, t), BF16)
    else:
        vv_spec = pl.BlockSpec((4, tm, LANES), lambda i: (0, i, 0))
        vv_shape = jax.ShapeDtypeStruct((4, t, LANES), BF16)
    return pl.pallas_call(
        functools.partial(_odd_in_kernel, transpose_v=transpose_v),
        grid=(t // tm,),
        in_specs=[pl.BlockSpec((tm, D_MODEL), row), pl.BlockSpec((1, D_MODEL), const),
                  pl.BlockSpec(w.shape, const), pl.BlockSpec((1, w.shape[1]), const),
                  pl.BlockSpec((tm, LANES), tab), pl.BlockSpec((tm, LANES), tab), pl.BlockSpec((tm, LANES), tab)],
        out_specs=[pl.BlockSpec((tm, nq), row), kv_spec, kv_spec,
                   pl.BlockSpec((4, tm, LANES), lambda i: (0, i, 0)), vv_spec],
        out_shape=[jax.ShapeDtypeStruct((t, nq), BF16), kv_shape, kv_shape,
                   jax.ShapeDtypeStruct((4, t, LANES), BF16), vv_shape],
        compiler_params=_params(1), name="odd_in",
    )(x, g, w, b, *tabs)


def _diff_lambda(lam_ref, lam_init):
    lp = lam_ref[...]
    return (jnp.exp(jnp.sum(lp[0:1] * lp[1:2], axis=-1, keepdims=True))
            - jnp.exp(jnp.sum(lp[2:3] * lp[3:4], axis=-1, keepdims=True)) + lam_init)


def _diff_prompt_kernel(q_ref, kz_ref, vt_ref, lam_ref, sub_ref, o_ref, m_ref, l_ref, acc_ref, *, lam_init):
    i = pl.program_id(2)
    tq = q_ref.shape[0]
    tk = tq
    q = q_ref[...]
    m_ref[...] = jnp.full(m_ref.shape, -jnp.inf, F32)
    l_ref[...] = jnp.zeros(l_ref.shape, F32)
    acc_ref[...] = jnp.zeros(acc_ref.shape, F32)

    def tile(j, mask):
        start = pl.multiple_of(j * tk, tk)
        vt = vt_ref[j]
        for mp in range(2):
            s = lax.dot_general(kz_ref[mp, pl.ds(start, tk), :], q, NT_DIMS,
                                preferred_element_type=F32)
            if mask is not None:
                s = jnp.where(mask, s, -jnp.inf)
            m_prev = m_ref[mp]
            m_new = jnp.maximum(m_prev, jnp.max(s, axis=0, keepdims=True))
            alpha = jnp.exp2(m_prev - m_new)
            p = jnp.exp2(s - m_new)
            l_ref[mp] = alpha * l_ref[mp] + jnp.sum(p, axis=0, keepdims=True)
            acc_ref[mp] = alpha * acc_ref[mp] + jnp.dot(vt, p.astype(BF16), preferred_element_type=F32)
            m_ref[mp] = m_new

    def pair(j2, carry):
        tile(2 * j2, None)
        tile(2 * j2 + 1, None)
        return carry

    lax.fori_loop(0, lax.shift_right_logical(i, 1), pair, 0)
    k_chunk = lax.broadcasted_iota(jnp.int32, (tk, tq), 0) // CHUNK
    q_chunk = lax.broadcasted_iota(jnp.int32, (tk, tq), 1) // CHUNK
    diag = k_chunk <= q_chunk
    odd = (i & 1) == 1

    @pl.when(odd)
    def _():
        tile(i - 1, None)
        tile(i, diag)

    @pl.when(jnp.logical_not(odd))
    def _():
        tile(i, diag)

    lam = _diff_lambda(lam_ref, lam_init)
    a = acc_ref[0] / l_ref[0] - lam * (acc_ref[1] / l_ref[1])
    a = a * lax.rsqrt(jnp.mean(a * a, axis=0, keepdims=True) + EPS) * sub_ref[...] * (1.0 - lam_init)
    o_ref[...] = a.T.astype(BF16)


def _diff_prompt(q, kz, vt, lam_p, sub_col, batch, seq, lam_init):
    t = q.shape[0]
    tq = ATTN_TILE
    nq = seq // tq
    assert vt.shape == (t // tq, A_WIDTH, tq)
    return pl.pallas_call(
        functools.partial(_diff_prompt_kernel, lam_init=lam_init),
        grid=(batch, A_HEADS, nq),
        in_specs=[pl.BlockSpec((tq, LANES), lambda b, h, i: (b * nq + i, h)),
                  pl.BlockSpec((2, seq, LANES), lambda b, h, i: (0, b, h)),
                  pl.BlockSpec((nq, LANES, tq), lambda b, h, i: (b, h, 0)),
                  pl.BlockSpec((4, A_DH), lambda b, h, i: (0, 0)),
                  pl.BlockSpec((LANES, 1), lambda b, h, i: (0, 0))],
        out_specs=pl.BlockSpec((tq, LANES), lambda b, h, i: (b * nq + i, h)),
        out_shape=jax.ShapeDtypeStruct((t, A_WIDTH), BF16),
        scratch_shapes=[pltpu.VMEM((2, 1, tq), F32), pltpu.VMEM((2, 1, tq), F32),
                        pltpu.VMEM((2, LANES, tq), F32)],
        compiler_params=_params(3), name="diff_attn_prompt",
    )(q, kz, vt, lam_p, sub_col)


def _diff_sample_kernel(q_ref, kn_ref, vn_ref, ckt_ref, cv_ref, lam_ref, sub_ref, o_ref, *, lam_init):
    sq = q_ref.shape[0]
    past = ckt_ref.shape[3]
    lo = lax.broadcasted_iota(jnp.int32, (sq, LANES), 1) < A_DH
    lam = _diff_lambda(lam_ref, lam_init)
    for h in range(A_HEADS):
        sl = slice(h * LANES, (h + 1) * LANES)
        q = q_ref[:, sl].astype(F32)
        q2 = jnp.concatenate([jnp.where(lo, q, 0.0), jnp.where(lo, 0.0, q)], axis=0).astype(BF16)
        s_c = jnp.dot(q2, ckt_ref[0, 0, sl, :].astype(BF16), preferred_element_type=F32)
        s_n = lax.dot_general(q2, kn_ref[:, sl], NT_DIMS, preferred_element_type=F32)
        m = jnp.maximum(jnp.max(s_c, axis=-1, keepdims=True), jnp.max(s_n, axis=-1, keepdims=True))
        e_c = jnp.exp2(s_c - m)
        e_n = jnp.exp2(s_n - m)
        den = jnp.sum(e_c, axis=-1, keepdims=True) + jnp.sum(e_n, axis=-1, keepdims=True)
        cv = cv_ref[0, 0, pl.ds(h, past, stride=A_HEADS), :].astype(BF16)
        o = (jnp.dot(e_c.astype(BF16), cv, preferred_element_type=F32)
             + jnp.dot(e_n.astype(BF16), vn_ref[:, sl], preferred_element_type=F32)) / den
        a = o[:sq] - lam * o[sq:]
        o_ref[:, sl] = (_rmsnorm_rows(a, sub_ref[...]) * (1.0 - lam_init)).astype(BF16)


def _diff_sample(q, kn, vn, cache_kt, cache_v, layer, lam_p, sub, lam_init):
    _, nb, _, past = cache_kt.shape
    t = q.shape[0]
    sq = t // nb
    row = lambda b: (b, 0)
    return pl.pallas_call(
        functools.partial(_diff_sample_kernel, lam_init=lam_init),
        grid=(nb,),
        in_specs=[pl.BlockSpec((sq, A_WIDTH), row), pl.BlockSpec((sq, A_WIDTH), row), pl.BlockSpec((sq, A_WIDTH), row),
                  pl.BlockSpec((1, 1, A_WIDTH, past), lambda b: (layer, b, 0, 0)),
                  pl.BlockSpec((1, 1, past * A_HEADS, LANES), lambda b: (layer, b, 0, 0)),
                  pl.BlockSpec((4, A_DH), lambda b: (0, 0)),
                  pl.BlockSpec((1, LANES), lambda b: (0, 0))],
        out_specs=pl.BlockSpec((sq, A_WIDTH), row),
        out_shape=jax.ShapeDtypeStruct((t, A_WIDTH), BF16),
        compiler_params=_params(1), name="diff_attn_sample",
    )(q, kn, vn, cache_kt, cache_v, lam_p, sub)


def _even_out_kernel(a_ref, u_ref, hist_ref, x_ref, pw_ref, ps_ref, w_ref, o_ref, ext_ref, *, n_hist):
    i = pl.program_id(1)
    tm = u_ref.shape[0]

    @pl.when(i == 0)
    def _():
        ext_ref[0:HIST_ROWS, :] = hist_ref[0]

    u = u_ref[...]
    ext_ref[HIST_ROWS:HIST_ROWS + tm, :] = u
    pos = (i * tm + lax.broadcasted_iota(jnp.int32, (tm, 1), 0)).astype(F32)
    acc = x_ref[...] + jnp.dot(a_ref[...], w_ref[0:A_WIDTH, :], preferred_element_type=F32)
    for g, win in enumerate(POOL_WINDOWS):
        sl = slice(g * POOL_CH, (g + 1) * POOL_CH)
        tot = u[:, sl]
        for back in range(1, win):
            tot = tot + ext_ref[HIST_ROWS - back:HIST_ROWS - back + tm, sl]
        cnt = jnp.minimum(float(win), pos + (1.0 + n_hist))
        pooled = (tot / cnt - u[:, sl]).astype(BF16)
        y = jnp.dot(pooled, pw_ref[g], preferred_element_type=F32) * ps_ref[:, sl]
        acc = acc + jnp.dot(y.astype(BF16), w_ref[A_WIDTH + g * POOL_CH:A_WIDTH + (g + 1) * POOL_CH, :],
                            preferred_element_type=F32)
    o_ref[...] = acc
    ext_ref[0:HIST_ROWS, :] = ext_ref[tm:tm + HIST_ROWS, :]


def _even_out(a, u, hist, x, pool_w, pool_scale, w_out, nseq, tm, n_hist):
    t = x.shape[0]
    nt = t // nseq // tm
    row = lambda b, i: (b * nt + i, 0)
    return pl.pallas_call(
        functools.partial(_even_out_kernel, n_hist=n_hist),
        grid=(nseq, nt),
        in_specs=[pl.BlockSpec((tm, A_WIDTH), row), pl.BlockSpec((tm, B_WIDTH), row),
                  pl.BlockSpec((1, HIST_ROWS, B_WIDTH), lambda b, i: (b, 0, 0)),
                  pl.BlockSpec((tm, D_MODEL), row),
                  pl.BlockSpec(pool_w.shape, lambda b, i: (0, 0, 0)),
                  pl.BlockSpec((1, B_WIDTH), lambda b, i: (0, 0)),
                  pl.BlockSpec(w_out.shape, lambda b, i: (0, 0))],
        out_specs=pl.BlockSpec((tm, D_MODEL), row),
        out_shape=jax.ShapeDtypeStruct((t, D_MODEL), F32),
        scratch_shapes=[pltpu.VMEM((HIST_ROWS + tm, B_WIDTH), F32)],
        compiler_params=_params(2), name="even_out",
    )(a, u, hist, x, pool_w, pool_scale, w_out)


def _swa_prompt_kernel(sink_ref, q_ref, kp_ref, kc_ref, vp_ref, vc_ref, o_ref):
    i = pl.program_id(1)
    tq = q_ref.shape[0]
    nk = 2 * tq
    nq = 4 * tq
    kc = lax.broadcasted_iota(jnp.int32, (2 * nk, nq), 0) % nk // CHUNK
    qc = lax.broadcasted_iota(jnp.int32, (2 * nk, nq), 1) % tq // CHUNK
    vis = (kc >= qc) & (kc <= qc + 2) & ((kc >= 2) | (i > 0))
    pair = lax.broadcasted_iota(jnp.int32, (1, nq), 1) // tq
    low_rows = lax.broadcasted_iota(jnp.int32, (LANES, 1), 0) < C_DH
    for kv in range(C_KV):
        qs = jnp.concatenate([q_ref[:, (kv * 4 + p) * LANES:(kv * 4 + p + 1) * LANES] for p in range(4)], axis=0)
        ks = jnp.concatenate([kp_ref[2 * kv], kc_ref[2 * kv], kp_ref[2 * kv + 1], kc_ref[2 * kv + 1]], axis=0)
        vt = jnp.concatenate([vp_ref[2 * kv], vc_ref[2 * kv], vp_ref[2 * kv + 1], vc_ref[2 * kv + 1]], axis=1)
        s = lax.dot_general(ks, qs, NT_DIMS, preferred_element_type=F32)
        s = jnp.where(vis, s, -jnp.inf)
        es, rdens = [], []
        for half in range(2):
            sink = jnp.zeros((1, nq), F32)
            for p in range(4):
                sink = jnp.where(pair == p, sink_ref[kv * 8 + 2 * p + half] * LOG2E, sink)
            sh = s[half * nk:(half + 1) * nk]
            m = jnp.maximum(jnp.max(sh, axis=0, keepdims=True), sink)
            e = jnp.exp2(sh - m)
            rdens.append(1.0 / (jnp.sum(e, axis=0, keepdims=True) + jnp.exp2(sink - m)))
            es.append(e.astype(BF16))
        o = jnp.dot(vt, jnp.concatenate(es, axis=0), preferred_element_type=F32)
        o = (o * jnp.where(low_rows, rdens[0], rdens[1])).T
        for pr in range(4):
            o_ref[:, (kv * 4 + pr) * LANES:(kv * 4 + pr + 1) * LANES] = o[pr * tq:(pr + 1) * tq].astype(BF16)


def _swa_prompt(sinks, q, kk, vvt, batch, seq):
    t = q.shape[0]
    tq = SWA_TILE
    nt = seq // tq
    kspec = lambda f: pl.BlockSpec((4, tq, LANES), f)
    vspec = lambda f: pl.BlockSpec((4, LANES, tq), f)
    return pl.pallas_call(
        _swa_prompt_kernel,
        grid=(batch, nt),
        in_specs=[pl.BlockSpec(memory_space=pltpu.SMEM),
                  pl.BlockSpec((tq, C_HEADS * C_DH), lambda b, i: (b * nt + i, 0)),
                  kspec(lambda b, i: (0, b * nt + jnp.maximum(i - 1, 0), 0)), kspec(lambda b, i: (0, b * nt + i, 0)),
                  vspec(lambda b, i: (0, 0, b * nt + jnp.maximum(i - 1, 0))), vspec(lambda b, i: (0, 0, b * nt + i))],
        out_specs=pl.BlockSpec((tq, C_HEADS * C_DH), lambda b, i: (b * nt + i, 0)),
        out_shape=jax.ShapeDtypeStruct((t, C_HEADS * C_DH), BF16),
        compiler_params=_params(2), name="swa_prompt",
    )(sinks, q, kk, kk, vvt, vvt)


def _swa_sample_kernel(sink_ref, q_ref, kk_ref, vv_ref, ck_ref, cv_ref, o_ref):
    sq = q_ref.shape[0]
    nc = ck_ref.shape[1]
    lo = lax.broadcasted_iota(jnp.int32, (nc, LANES), 1) < C_DH

    def halves(c, kv):
        own = jnp.where(lo, c, 0.0) if kv == 0 else jnp.where(lo, 0.0, c)
        swapped = pltpu.roll(own, C_DH, 1)
        return ((own, swapped) if kv == 0 else (swapped, own))

    ck = ck_ref[0]
    cv = cv_ref[0]
    for kv in range(C_KV):
        qs = jnp.concatenate([q_ref[:, (kv * 4 + p) * LANES:(kv * 4 + p + 1) * LANES] for p in range(4)], axis=0)
        ckh = halves(ck, kv)
        cvh = halves(cv, kv)
        o = jnp.zeros((4 * sq, LANES), F32)
        for half in range(2):
            ks = jnp.concatenate([ckh[half].astype(BF16), kk_ref[2 * kv + half]], axis=0)
            vs = jnp.concatenate([cvh[half].astype(BF16), vv_ref[2 * kv + half]], axis=0)
            s = lax.dot_general(qs, ks, NT_DIMS, preferred_element_type=F32)
            sink = jnp.concatenate(
                [jnp.full((sq, 1), sink_ref[kv * 8 + 2 * p + half] * LOG2E, F32) for p in range(4)], axis=0)
            m = jnp.maximum(jnp.max(s, axis=-1, keepdims=True), sink)
            e = jnp.exp2(s - m)
            p = e / (jnp.sum(e, axis=-1, keepdims=True) + jnp.exp2(sink - m))
            o = o + jnp.dot(p.astype(BF16), vs, preferred_element_type=F32)
        for pr in range(4):
            o_ref[:, (kv * 4 + pr) * LANES:(kv * 4 + pr + 1) * LANES] = o[pr * sq:(pr + 1) * sq].astype(BF16)


def _swa_sample(sinks, q, kk, vv, cache_k, cache_v):
    nb, nc, _ = cache_k.shape
    t = q.shape[0]
    sq = t // nb
    return pl.pallas_call(
        _swa_sample_kernel,
        grid=(nb,),
        in_specs=[pl.BlockSpec(memory_space=pltpu.SMEM),
                  pl.BlockSpec((sq, C_HEADS * C_DH), lambda b: (b, 0)),
                  pl.BlockSpec((4, sq, LANES), lambda b: (0, b, 0)),
                  pl.BlockSpec((4, sq, LANES), lambda b: (0, b, 0)),
                  pl.BlockSpec((1, nc, LANES), lambda b: (b, 0, 0)),
                  pl.BlockSpec((1, nc, LANES), lambda b: (b, 0, 0))],
        out_specs=pl.BlockSpec((sq, C_HEADS * C_DH), lambda b: (b, 0)),
        out_shape=jax.ShapeDtypeStruct((t, C_HEADS * C_DH), BF16),
        compiler_params=_params(1), name="swa_sample",
    )(sinks, q, kk, vv, cache_k, cache_v)


def _odd_out_kernel(o_ref, x_ref, w_ref, y_ref):
    y_ref[...] = x_ref[...] + jnp.dot(o_ref[...], w_ref[...], preferred_element_type=F32)


def _odd_out(o, x, w, tm):
    t = x.shape[0]
    row = lambda i: (i, 0)
    return pl.pallas_call(
        _odd_out_kernel,
        grid=(t // tm,),
        in_specs=[pl.BlockSpec((tm, o.shape[1]), row), pl.BlockSpec((tm, D_MODEL), row),
                  pl.BlockSpec(w.shape, lambda i: (0, 0))],
        out_specs=pl.BlockSpec((tm, D_MODEL), row),
        out_shape=jax.ShapeDtypeStruct((t, D_MODEL), F32),
        compiler_params=_params(1), name="odd_out",
    )(o, x, w)


def _shift_rows(up, prev, shift, nseq):
    seq_len = up.shape[0] // nseq
    rolled = pltpu.roll(up, shift, 0)
    first = lax.broadcasted_iota(jnp.int32, (SUBLANES, up.shape[1]), 0) < shift
    pieces = []
    for s in range(nseq):
        lo = s * seq_len
        head = jnp.where(first, pltpu.roll(prev[s], shift, 0), pltpu.roll(up[lo:lo + SUBLANES], shift, 0))
        pieces += [head, rolled[lo + SUBLANES:lo + seq_len]]
    return jnp.concatenate(pieces, axis=0)


def _ffn_kernel(x_ref, g_ref, hist_ref, wu_ref, cw_ref, cb_ref, wd_ref, gf_ref,
                y_ref, st_ref, hn_ref, h_ref, carry_ref, *, nseq, final_norm):
    i = pl.program_id(1)
    tm = x_ref.shape[0]
    seq_len = tm // nseq

    @pl.when(i == 0)
    def _():
        carry_ref[...] = hist_ref[0]

    x = x_ref[...]
    hn_ref[...] = _rmsnorm_rows(x, g_ref[...]).astype(BF16)
    for c in range(N_FF_CHUNKS):
        conv = []
        for sl in (slice(c * FF_CHUNK, (c + 1) * FF_CHUNK), slice(D_FF + c * FF_CHUNK, D_FF + (c + 1) * FF_CHUNK)):
            up = jnp.dot(hn_ref[...], wu_ref[0, :, sl], preferred_element_type=F32)
            prev = [carry_ref[s, :, sl] for s in range(nseq)]
            conv.append(cb_ref[0, :, sl] + _shift_rows(up, prev, 2, nseq) * cw_ref[0, 0:1, sl]
                        + _shift_rows(up, prev, 1, nseq) * cw_ref[0, 1:2, sl] + up * cw_ref[0, 2:3, sl])
            for s in range(nseq):
                carry_ref[s, :, sl] = up[(s + 1) * seq_len - SUBLANES:(s + 1) * seq_len]
        gate, val = conv
        h_ref[:, c * FF_CHUNK:(c + 1) * FF_CHUNK] = (gate * jax.nn.sigmoid(gate) * val).astype(BF16)
    y = x + jnp.dot(h_ref[...], wd_ref[0], preferred_element_type=F32)
    if final_norm:
        y = _rmsnorm_rows(y, gf_ref[...])
    y_ref[...] = y
    st_ref[0] = carry_ref[...]


def _ffn(x, g, hist, wu, cw, cb, wd, layer, g_final, nseq, tm, final_norm):
    t = x.shape[0]
    n_outer = hist.shape[0]
    nt = t // n_outer // tm
    row = lambda b, i: (b * nt + i, 0)
    c2 = lambda b, i: (0, 0)
    lay = lambda b, i: (layer, 0, 0)
    st_block = (1,) + hist.shape[1:]
    st_map = lambda b, i: (b, 0, 0, 0)
    return pl.pallas_call(
        functools.partial(_ffn_kernel, nseq=nseq, final_norm=final_norm),
        grid=(n_outer, nt),
        in_specs=[pl.BlockSpec((tm, D_MODEL), row), pl.BlockSpec((1, D_MODEL), c2),
                  pl.BlockSpec(st_block, st_map),
                  pl.BlockSpec((1,) + wu.shape[1:], lay), pl.BlockSpec((1,) + cw.shape[1:], lay),
                  pl.BlockSpec((1,) + cb.shape[1:], lay), pl.BlockSpec((1,) + wd.shape[1:], lay),
                  pl.BlockSpec((1, D_MODEL), c2)],
        out_specs=[pl.BlockSpec((tm, D_MODEL), row), pl.BlockSpec(st_block, st_map)],
        out_shape=[jax.ShapeDtypeStruct((t, D_MODEL), F32), jax.ShapeDtypeStruct(hist.shape, F32)],
        scratch_shapes=[pltpu.VMEM((tm, D_MODEL), BF16),
                        pltpu.VMEM((tm, D_FF), BF16),
                        pltpu.VMEM(hist.shape[1:], F32)],
        compiler_params=_params(2), name="conv_ffn",
    )(x, g, hist, wu, cw, cb, wd, g_final)


def kernel(x_prompt, x_sample, cache_diff_k, cache_diff_v, state_pool, cache_swa_k, cache_swa_v, state_ffn_conv,
           norm_attn, norm_ffn, norm_final, w_in_even, w_out_even, diff_lambda, diff_subln, pool_w, pool_scale,
           w_in_odd, b_in_odd, w_out_odd, sinks, w_up, conv_w, conv_b, w_down):
    bp, sp, d = x_prompt.shape
    bs, ss, _ = x_sample.shape
    depth = norm_attn.shape[0]
    past = cache_diff_k.shape[2]
    hp = x_prompt.reshape(bp * sp, d)
    hs = x_sample.reshape(bs * ss, d)
    tm_s = bs * ss

    tabs_p = _rope_tables(jnp.arange(sp, dtype=jnp.int32))
    tabs_s = tuple(jnp.tile(t, (bs, 1)) for t in _rope_tables(past + jnp.arange(ss, dtype=jnp.int32)))

    wu_b = w_up.astype(BF16)
    wd_b = w_down.astype(BF16)
    cb3 = conv_b[:, None, :]
    conv_pad = jnp.zeros((bs, SUBLANES - (CONV_W - 1), 2 * D_FF), F32)
    zero_conv = jnp.zeros((bp, 1, SUBLANES, 2 * D_FF), F32)
    zero_pool = jnp.zeros((bp, HIST_ROWS, B_WIDTH), F32)
    n_even = cache_diff_k.shape[0]
    cache_kt = jnp.transpose(cache_diff_k, (0, 1, 3, 4, 5, 2)).reshape(n_even, bs, A_WIDTH, past)
    cache_vr = cache_diff_v.reshape(n_even, bs, past * A_HEADS, 2 * A_DH)

    dkp, dvp, plp, skp, svp, fcp = [], [], [], [], [], []
    dks, dvs, pls, sks, svs, fcs = [], [], [], [], [], []
    for i in range(depth):
        j = i // 2
        g_attn = norm_attn[i][None, :]
        if i % 2 == 0:
            lam_init = 0.8 - 0.6 * math.exp(-0.3 * i)
            w_in = w_in_even[j].astype(BF16)
            w_out = w_out_even[j].astype(BF16)
            pw = pool_w[j].astype(BF16)
            ps = pool_scale[j][None, :]
            lam_p = diff_lambda[j]

            q, kt, kz, v, vt, u = _even_in(hp, g_attn, w_in, tabs_p, ROW_TILE, bp, True)
            a = _diff_prompt(q, kz, vt, lam_p, diff_subln[j][:, None], bp, sp, lam_init)
            hp = _even_out(a, u, zero_pool, hp, pw, ps, w_out, bp, ROW_TILE, 0)
            dkp.append(kt)
            dvp.append(v.reshape(bp, sp, A_HEADS, 2 * A_DH))
            plp.append(u.reshape(bp, sp, B_WIDTH)[:, sp - POOL_HIST:])

            q, k, kb, v, vb, u = _even_in(hs, g_attn, w_in, tabs_s, tm_s, 1, False)
            a = _diff_sample(q, kb, vb, cache_kt, cache_vr, j, lam_p, diff_subln[j][None, :], lam_init)
            hist = jnp.concatenate([jnp.zeros((bs, HIST_ROWS - POOL_HIST, B_WIDTH), F32), state_pool[j]], axis=1)
            hs = _even_out(a, u, hist, hs, pw, ps, w_out, bs, ss, POOL_HIST)
            dks.append(k.reshape(bs, ss, A_HEADS, 2, A_DH))
            dvs.append(v.reshape(bs, ss, A_HEADS, 2 * A_DH))
            pls.append(u.reshape(bs, ss, B_WIDTH)[:, ss - POOL_HIST:])
        else:
            w_in = w_in_odd[j].astype(BF16)
            b_in = b_in_odd[j][None, :]
            w_out = w_out_odd[j].astype(BF16)
            sk = sinks[j]

            q, k, v, kk, vvt = _odd_in(hp, g_attn, w_in, b_in, tabs_p, ROW_TILE, bp, C_CACHE, True)
            o = _swa_prompt(sk, q, kk, vvt, bp, sp)
            hp = _odd_out(o, hp, w_out, ROW_TILE)
            skp.append(k.reshape(bp, C_CACHE, C_KV, C_DH))
            svp.append(v.reshape(bp, C_CACHE, C_KV, C_DH))

            q, k, v, kk, vv = _odd_in(hs, g_attn, w_in, b_in, tabs_s, tm_s, 1, tm_s, False)
            o = _swa_sample(sk, q, kk, vv, cache_swa_k[j].reshape(bs, C_CACHE, C_KV * C_DH),
                            cache_swa_v[j].reshape(bs, C_CACHE, C_KV * C_DH))
            hs = _odd_out(o, hs, w_out, tm_s)
            k_all = jnp.concatenate([cache_swa_k[j], k.reshape(bs, ss, C_KV, C_DH)], axis=1)
            v_all = jnp.concatenate([cache_swa_v[j], v.reshape(bs, ss, C_KV, C_DH)], axis=1)
            sks.append(k_all[:, -C_CACHE:])
            svs.append(v_all[:, -C_CACHE:])

        last = i == depth - 1
        g_ffn = norm_ffn[i][None, :]
        g_fin = norm_final[None, :]
        hp, st = _ffn(hp, g_ffn, zero_conv, wu_b, conv_w, cb3, wd_b, i, g_fin, 1, ROW_TILE, last)
        fcp.append(st[:, 0, SUBLANES - (CONV_W - 1):])
        hist = jnp.concatenate([conv_pad, state_ffn_conv[i]], axis=1)[None]
        hs, st = _ffn(hs, g_ffn, hist, wu_b, conv_w, cb3, wd_b, i, g_fin, bs, tm_s, last)
        fcs.append(st[0, :, SUBLANES - (CONV_W - 1):])

    diff_k_prompt = jnp.transpose(jnp.stack(dkp).reshape(n_even, bp, A_HEADS, 2, A_DH, sp), (0, 1, 5, 2, 3, 4))
    return (hp.reshape(bp, sp, d), hs.reshape(bs, ss, d),
            diff_k_prompt, jnp.stack(dvp), jnp.stack(plp), jnp.stack(skp), jnp.stack(svp), jnp.stack(fcp),
            jnp.stack(dks), jnp.stack(dvs), jnp.stack(pls), jnp.stack(sks), jnp.stack(svs), jnp.stack(fcs))
```

```python
import functools
import math

import jax
import jax.numpy as jnp
from jax import lax
from jax.experimental import pallas as pl
from jax.experimental.pallas import tpu as pltpu

F32 = jnp.float32
BF16 = jnp.bfloat16

D_MODEL = 1024
CHUNK = 64
ROPE_THETA = 10000.0
EPS = 1e-5
A_HEADS = 4
A_DH = 64
A_WIDTH = A_HEADS * 2 * A_DH
POOL_WINDOWS = (2, 4, 8, 16)
POOL_CH = 128
POOL_HIST = 15
B_WIDTH = 512
C_HEADS = 16
C_KV = 2
C_DH = 64
C_CACHE = 128
D_FF = 2816
CONV_W = 3

LOG2E = math.log2(math.e)
LANES = 128
SUBLANES = 8
HIST_ROWS = 16
FF_CHUNK = 256
N_FF_CHUNKS = D_FF // FF_CHUNK
ROW_TILE = 512
ATTN_TILE = 512
SWA_TILE = 2 * CHUNK
VMEM_LIMIT = 56 * 1024 * 1024

NT_DIMS = (((1,), (1,)), ((), ()))


def _params(n_axes, vmem=VMEM_LIMIT):
    return pltpu.CompilerParams(dimension_semantics=("arbitrary",) * n_axes, vmem_limit_bytes=vmem)


def _rmsnorm_rows(x, g):
    return x * lax.rsqrt(jnp.mean(x * x, axis=-1, keepdims=True) + EPS) * g


def _rope128(z, cos, s_up, s_dn):
    return z * cos + pltpu.roll(z, 96, 1) * s_up + pltpu.roll(z, 32, 1) * s_dn


def _rope_tables(pos):
    inv = ROPE_THETA ** (-jnp.arange(0, A_DH, 2, dtype=F32) / A_DH)
    ang = pos.astype(F32)[:, None] * inv[None, :]
    cos, sin = jnp.cos(ang), jnp.sin(ang)
    zero = jnp.zeros_like(sin)
    cos128 = jnp.tile(cos, (1, 4))
    s_up = jnp.tile(jnp.concatenate([-sin, zero], axis=1), (1, 2))
    s_dn = jnp.tile(jnp.concatenate([zero, sin], axis=1), (1, 2))
    return cos128, s_up, s_dn


def _even_in_kernel(x_ref, g_ref, w_ref, cos_ref, sup_ref, sdn_ref,
                    q_ref, k_ref, kz_ref, v_ref, vx_ref, u_ref, *, prompt):
    tm = x_ref.shape[0]
    hn = _rmsnorm_rows(x_ref[...], g_ref[...]).astype(BF16)
    z = jnp.dot(hn, w_ref[...], preferred_element_type=F32)
    cos, s_up, s_dn = cos_ref[...], sup_ref[...], sdn_ref[...]
    lo = lax.broadcasted_iota(jnp.int32, (tm, LANES), 1) < A_DH
    for c in range(A_HEADS):
        sl = slice(c * LANES, (c + 1) * LANES)
        rq = _rope128(z[:, sl], cos, s_up, s_dn)
        q_ref[:, sl] = (rq * (A_DH ** -0.5 * LOG2E)).astype(BF16)
        rk = _rope128(z[:, A_WIDTH + c * LANES:A_WIDTH + (c + 1) * LANES], cos, s_up, s_dn)
        if prompt:
            k_ref[0, sl, :] = rk.T
            kz_ref[0, :, sl] = jnp.where(lo, rk, 0.0).astype(BF16)
            kz_ref[1, :, sl] = jnp.where(lo, 0.0, rk).astype(BF16)
        else:
            k_ref[:, sl] = rk
            kz_ref[:, sl] = rk.astype(BF16)
    v = z[:, 2 * A_WIDTH:3 * A_WIDTH]
    v_ref[...] = v
    if prompt:
        vx_ref[0] = v.T.astype(BF16)
    else:
        vx_ref[...] = v.astype(BF16)
    u_ref[...] = z[:, 3 * A_WIDTH:]


def _even_in(x, g, w, tabs, tm, nseq, prompt):
    t = x.shape[0]
    n_tab = tabs[0].shape[0] // tm
    nt = t // nseq // tm
    row = lambda i: (i, 0)
    tab = lambda i: (i % n_tab, 0)
    const = lambda i: (0, 0)
    if prompt:
        k_spec = pl.BlockSpec((1, A_WIDTH, tm), lambda i: (i // nt, 0, i % nt))
        k_shape = jax.ShapeDtypeStruct((nseq, A_WIDTH, t // nseq), F32)
        kz_spec = pl.BlockSpec((2, tm, A_WIDTH), lambda i: (0, i, 0))
        kz_shape = jax.ShapeDtypeStruct((2, t, A_WIDTH), BF16)
        vx_spec = pl.BlockSpec((1, A_WIDTH, tm), lambda i: (i, 0, 0))
        vx_shape = jax.ShapeDtypeStruct((t // tm, A_WIDTH, tm), BF16)
    else:
        k_spec = pl.BlockSpec((tm, A_WIDTH), row)
        k_shape = jax.ShapeDtypeStruct((t, A_WIDTH), F32)
        kz_spec = pl.BlockSpec((tm, A_WIDTH), row)
        kz_shape = jax.ShapeDtypeStruct((t, A_WIDTH), BF16)
        vx_spec = pl.BlockSpec((tm, A_WIDTH), row)
        vx_shape = jax.ShapeDtypeStruct((t, A_WIDTH), BF16)
    return pl.pallas_call(
        functools.partial(_even_in_kernel, prompt=prompt),
        grid=(t // tm,),
        in_specs=[pl.BlockSpec((tm, D_MODEL), row), pl.BlockSpec((1, D_MODEL), const),
                  pl.BlockSpec(w.shape, const),
                  pl.BlockSpec((tm, LANES), tab), pl.BlockSpec((tm, LANES), tab), pl.BlockSpec((tm, LANES), tab)],
        out_specs=[pl.BlockSpec((tm, A_WIDTH), row), k_spec, kz_spec,
                   pl.BlockSpec((tm, A_WIDTH), row), vx_spec,
                   pl.BlockSpec((tm, B_WIDTH), row)],
        out_shape=[jax.ShapeDtypeStruct((t, A_WIDTH), BF16), k_shape, kz_shape,
                   jax.ShapeDtypeStruct((t, A_WIDTH), F32), vx_shape,
                   jax.ShapeDtypeStruct((t, B_WIDTH), F32)],
        compiler_params=_params(1), name="even_in",
    )(x, g, w, *tabs)


def _odd_in_kernel(x_ref, g_ref, w_ref, b_ref, cos_ref, sup_ref, sdn_ref,
                   q_ref, k_ref, v_ref, kk_ref, vv_ref, *, transpose_v):
    tm = x_ref.shape[0]
    hn = _rmsnorm_rows(x_ref[...], g_ref[...]).astype(BF16)
    z = jnp.dot(hn, w_ref[...], preferred_element_type=F32) + b_ref[...]
    cos, s_up, s_dn = cos_ref[...], sup_ref[...], sdn_ref[...]
    nq = C_HEADS * C_DH
    for c in range(nq // LANES):
        sl = slice(c * LANES, (c + 1) * LANES)
        q_ref[:, sl] = (_rope128(z[:, sl], cos, s_up, s_dn) * (C_DH ** -0.5 * LOG2E)).astype(BF16)
    k = _rope128(z[:, nq:nq + LANES], cos, s_up, s_dn)
    v = z[:, nq + LANES:nq + 2 * LANES]
    tail = k_ref.shape[0]
    k_ref[...] = k[tm - tail:]
    v_ref[...] = v[tm - tail:]
    lo = lax.broadcasted_iota(jnp.int32, (tm, LANES), 1) < C_DH
    for src, dst, tr in ((k, kk_ref, False), (v, vv_ref, transpose_v)):
        h0 = jnp.where(lo, src, 0.0)
        h1 = jnp.where(lo, 0.0, src)
        for n, val in enumerate((h0, pltpu.roll(h0, C_DH, 1), pltpu.roll(h1, C_DH, 1), h1)):
            dst[n] = (val.T if tr else val).astype(BF16)


def _odd_in(x, g, w, b, tabs, tm, nseq, tail, transpose_v):
    t = x.shape[0]
    n_tab = tabs[0].shape[0] // tm
    nt = t // nseq // tm
    nq = C_HEADS * C_DH
    row = lambda i: (i, 0)
    kv_spec = pl.BlockSpec((tail, LANES), lambda i: (i // nt, 0))
    kv_shape = jax.ShapeDtypeStruct((nseq * tail, LANES), F32)
    tab = lambda i: (i % n_tab, 0)
    const = lambda i: (0, 0)
    if transpose_v:
        vv_spec = pl.BlockSpec((4, LANES, tm), lambda i: (0, 0, i))
        vv_shape = jax.ShapeDtypeStruct((4, LANES, t), BF16)
    else:
        vv_spec = pl.BlockSpec((4, tm, LANES), lambda i: (0, i, 0))
        vv_shape = jax.ShapeDtypeStruct((4, t, LANES), BF16)
    return pl.pallas_call(
        functools.partial(_odd_in_kernel, transpose_v=transpose_v),
        grid=(t // tm,),
        in_specs=[pl.BlockSpec((tm, D_MODEL), row), pl.BlockSpec((1, D_MODEL), const),
                  pl.BlockSpec(w.shape, const), pl.BlockSpec((1, w.shape[1]), const),
                  pl.BlockSpec((tm, LANES), tab), pl.BlockSpec((tm, LANES), tab), pl.BlockSpec((tm, LANES), tab)],
        out_specs=[pl.BlockSpec((tm, nq), row), kv_spec, kv_spec,
                   pl.BlockSpec((4, tm, LANES), lambda i: (0, i, 0)), vv_spec],
        out_shape=[jax.ShapeDtypeStruct((t, nq), BF16), kv_shape, kv_shape,
                   jax.ShapeDtypeStruct((4, t, LANES), BF16), vv_shape],
        compiler_params=_params(1), name="odd_in",
    )(x, g, w, b, *tabs)


def _diff_lambda(lam_ref, lam_init):
    lp = lam_ref[...]
    return (jnp.exp(jnp.sum(lp[0:1] * lp[1:2], axis=-1, keepdims=True))
            - jnp.exp(jnp.sum(lp[2:3] * lp[3:4], axis=-1, keepdims=True)) + lam_init)


def _diff_prompt_kernel(q_ref, kz_ref, vt_ref, lam_ref, sub_ref, o_ref, m_ref, l_ref, acc_ref, sa_ref, sb_ref,
                        *, lam_init):
    i = pl.program_id(2)
    tq = q_ref.shape[0]
    tk = tq
    q = q_ref[...]
    m_ref[...] = jnp.full(m_ref.shape, -jnp.inf, F32)
    l_ref[...] = jnp.zeros(l_ref.shape, F32)
    acc_ref[...] = jnp.zeros(acc_ref.shape, F32)

    def scores(j, s_ref):
        start = pl.multiple_of(j * tk, tk)
        for mp in range(2):
            s_ref[mp] = lax.dot_general(kz_ref[mp, pl.ds(start, tk), :], q, NT_DIMS,
                                        preferred_element_type=F32)

    def consume(j, s_ref, mask):
        vt = vt_ref[j]
        for mp in range(2):
            s = s_ref[mp]
            if mask is not None:
                s = jnp.where(mask, s, -jnp.inf)
            m_prev = m_ref[mp]
            m_new = jnp.maximum(m_prev, jnp.max(s, axis=0, keepdims=True))
            alpha = jnp.exp2(m_prev - m_new)
            p = jnp.exp2(s - m_new)
            l_ref[mp] = alpha * l_ref[mp] + jnp.sum(p, axis=0, keepdims=True)
            acc_ref[mp] = alpha * acc_ref[mp] + jnp.dot(vt, p.astype(BF16), preferred_element_type=F32)
            m_ref[mp] = m_new

    scores(0, sa_ref)

    def pair(t, carry):
        scores(2 * t + 1, sb_ref)
        consume(2 * t, sa_ref, None)
        scores(2 * t + 2, sa_ref)
        consume(2 * t + 1, sb_ref, None)
        return carry

    lax.fori_loop(0, lax.shift_right_logical(i, 1), pair, 0)
    k_chunk = lax.broadcasted_iota(jnp.int32, (tk, tq), 0) // CHUNK
    q_chunk = lax.broadcasted_iota(jnp.int32, (tk, tq), 1) // CHUNK
    diag = k_chunk <= q_chunk
    odd = (i & 1) == 1

    @pl.when(odd)
    def _():
        scores(i, sb_ref)
        consume(i - 1, sa_ref, None)
        consume(i, sb_ref, diag)

    @pl.when(jnp.logical_not(odd))
    def _():
        consume(i, sa_ref, diag)

    lam = _diff_lambda(lam_ref, lam_init)
    a = acc_ref[0] / l_ref[0] - lam * (acc_ref[1] / l_ref[1])
    a = a * lax.rsqrt(jnp.mean(a * a, axis=0, keepdims=True) + EPS) * sub_ref[...] * (1.0 - lam_init)
    o_ref[...] = a.T.astype(BF16)


def _diff_prompt(q, kz, vt, lam_p, sub_col, batch, seq, lam_init):
    t = q.shape[0]
    tq = ATTN_TILE
    nq = seq // tq
    assert vt.shape == (t // tq, A_WIDTH, tq)
    return pl.pallas_call(
        functools.partial(_diff_prompt_kernel, lam_init=lam_init),
        grid=(batch, A_HEADS, nq),
        in_specs=[pl.BlockSpec((tq, LANES), lambda b, h, i: (b * nq + i, h)),
                  pl.BlockSpec((2, seq, LANES), lambda b, h, i: (0, b, h)),
                  pl.BlockSpec((nq, LANES, tq), lambda b, h, i: (b, h, 0)),
                  pl.BlockSpec((4, A_DH), lambda b, h, i: (0, 0)),
                  pl.BlockSpec((LANES, 1), lambda b, h, i: (0, 0))],
        out_specs=pl.BlockSpec((tq, LANES), lambda b, h, i: (b * nq + i, h)),
        out_shape=jax.ShapeDtypeStruct((t, A_WIDTH), BF16),
        scratch_shapes=[pltpu.VMEM((2, 1, tq), F32), pltpu.VMEM((2, 1, tq), F32),
                        pltpu.VMEM((2, LANES, tq), F32),
                        pltpu.VMEM((2, tq, tq), F32), pltpu.VMEM((2, tq, tq), F32)],
        compiler_params=_params(3), name="diff_attn_prompt",
    )(q, kz, vt, lam_p, sub_col)


def _diff_sample_kernel(q_ref, kn_ref, vn_ref, ckt_ref, cv_ref, lam_ref, sub_ref, o_ref, *, lam_init):
    sq = q_ref.shape[0]
    past = ckt_ref.shape[3]
    lo = lax.broadcasted_iota(jnp.int32, (sq, LANES), 1) < A_DH
    lam = _diff_lambda(lam_ref, lam_init)
    for h in range(A_HEADS):
        sl = slice(h * LANES, (h + 1) * LANES)
        q = q_ref[:, sl].astype(F32)
        q2 = jnp.concatenate([jnp.where(lo, q, 0.0), jnp.where(lo, 0.0, q)], axis=0).astype(BF16)
        s_c = jnp.dot(q2, ckt_ref[0, 0, sl, :].astype(BF16), preferred_element_type=F32)
        s_n = lax.dot_general(q2, kn_ref[:, sl], NT_DIMS, preferred_element_type=F32)
        m = jnp.maximum(jnp.max(s_c, axis=-1, keepdims=True), jnp.max(s_n, axis=-1, keepdims=True))
        e_c = jnp.exp2(s_c - m)
        e_n = jnp.exp2(s_n - m)
        den = jnp.sum(e_c, axis=-1, keepdims=True) + jnp.sum(e_n, axis=-1, keepdims=True)
        cv = cv_ref[0, 0, pl.ds(h, past, stride=A_HEADS), :].astype(BF16)
        o = (jnp.dot(e_c.astype(BF16), cv, preferred_element_type=F32)
             + jnp.dot(e_n.astype(BF16), vn_ref[:, sl], preferred_element_type=F32)) / den
        a = o[:sq] - lam * o[sq:]
        o_ref[:, sl] = (_rmsnorm_rows(a, sub_ref[...]) * (1.0 - lam_init)).astype(BF16)


def _diff_sample(q, kn, vn, cache_kt, cache_v, layer, lam_p, sub, lam_init):
    _, nb, _, past = cache_kt.shape
    t = q.shape[0]
    sq = t // nb
    row = lambda b: (b, 0)
    return pl.pallas_call(
        functools.partial(_diff_sample_kernel, lam_init=lam_init),
        grid=(nb,),
        in_specs=[pl.BlockSpec((sq, A_WIDTH), row), pl.BlockSpec((sq, A_WIDTH), row), pl.BlockSpec((sq, A_WIDTH), row),
                  pl.BlockSpec((1, 1, A_WIDTH, past), lambda b: (layer, b, 0, 0)),
                  pl.BlockSpec((1, 1, past * A_HEADS, LANES), lambda b: (layer, b, 0, 0)),
                  pl.BlockSpec((4, A_DH), lambda b: (0, 0)),
                  pl.BlockSpec((1, LANES), lambda b: (0, 0))],
        out_specs=pl.BlockSpec((sq, A_WIDTH), row),
        out_shape=jax.ShapeDtypeStruct((t, A_WIDTH), BF16),
        compiler_params=_params(1), name="diff_attn_sample",
    )(q, kn, vn, cache_kt, cache_v, lam_p, sub)


def _even_out_kernel(a_ref, u_ref, hist_ref, x_ref, pw_ref, ps_ref, w_ref, o_ref, ext_ref, *, n_hist):
    i = pl.program_id(1)
    tm = u_ref.shape[0]

    @pl.when(i == 0)
    def _():
        ext_ref[0:HIST_ROWS, :] = hist_ref[0]

    u = u_ref[...]
    ext_ref[HIST_ROWS:HIST_ROWS + tm, :] = u
    pos = (i * tm + lax.broadcasted_iota(jnp.int32, (tm, 1), 0)).astype(F32)
    acc = x_ref[...] + jnp.dot(a_ref[...], w_ref[0:A_WIDTH, :], preferred_element_type=F32)
    for g, win in enumerate(POOL_WINDOWS):
        sl = slice(g * POOL_CH, (g + 1) * POOL_CH)
        tot = u[:, sl]
        for back in range(1, win):
            tot = tot + ext_ref[HIST_ROWS - back:HIST_ROWS - back + tm, sl]
        cnt = jnp.minimum(float(win), pos + (1.0 + n_hist))
        pooled = (tot / cnt - u[:, sl]).astype(BF16)
        y = jnp.dot(pooled, pw_ref[g], preferred_element_type=F32) * ps_ref[:, sl]
        acc = acc + jnp.dot(y.astype(BF16), w_ref[A_WIDTH + g * POOL_CH:A_WIDTH + (g + 1) * POOL_CH, :],
                            preferred_element_type=F32)
    o_ref[...] = acc
    ext_ref[0:HIST_ROWS, :] = ext_ref[tm:tm + HIST_ROWS, :]


def _even_out(a, u, hist, x, pool_w, pool_scale, w_out, nseq, tm, n_hist):
    t = x.shape[0]
    nt = t // nseq // tm
    row = lambda b, i: (b * nt + i, 0)
    return pl.pallas_call(
        functools.partial(_even_out_kernel, n_hist=n_hist),
        grid=(nseq, nt),
        in_specs=[pl.BlockSpec((tm, A_WIDTH), row), pl.BlockSpec((tm, B_WIDTH), row),
                  pl.BlockSpec((1, HIST_ROWS, B_WIDTH), lambda b, i: (b, 0, 0)),
                  pl.BlockSpec((tm, D_MODEL), row),
                  pl.BlockSpec(pool_w.shape, lambda b, i: (0, 0, 0)),
                  pl.BlockSpec((1, B_WIDTH), lambda b, i: (0, 0)),
                  pl.BlockSpec(w_out.shape, lambda b, i: (0, 0))],
        out_specs=pl.BlockSpec((tm, D_MODEL), row),
        out_shape=jax.ShapeDtypeStruct((t, D_MODEL), F32),
        scratch_shapes=[pltpu.VMEM((HIST_ROWS + tm, B_WIDTH), F32)],
        compiler_params=_params(2), name="even_out",
    )(a, u, hist, x, pool_w, pool_scale, w_out)


def _swa_bias(tq):
    nk, nq = 2 * tq, 4 * tq
    kc = (jnp.arange(2 * nk) % nk // CHUNK)[:, None]
    qc = (jnp.arange(nq) % tq // CHUNK)[None, :]
    band = (kc >= qc) & (kc <= qc + 2)
    return jnp.where(jnp.stack([band & (kc >= 2), band]), 0.0, -jnp.inf).astype(F32)


def _swa_prompt_kernel(sink_ref, q_ref, kp_ref, kc_ref, vp_ref, vc_ref, bias_ref, o_ref, sa_ref, sb_ref):
    tq = q_ref.shape[0]
    nk = 2 * tq
    nq = 4 * tq
    pair = lax.broadcasted_iota(jnp.int32, (1, nq), 1) // tq
    low_rows = lax.broadcasted_iota(jnp.int32, (LANES, 1), 0) < C_DH

    def scores(kv, s_ref):
        qs = jnp.concatenate([q_ref[:, (kv * 4 + p) * LANES:(kv * 4 + p + 1) * LANES] for p in range(4)], axis=0)
        ks = jnp.concatenate([kp_ref[2 * kv], kc_ref[2 * kv], kp_ref[2 * kv + 1], kc_ref[2 * kv + 1]], axis=0)
        s_ref[...] = lax.dot_general(ks, qs, NT_DIMS, preferred_element_type=F32)

    def attend(kv, s_ref):
        vt = jnp.concatenate([vp_ref[2 * kv], vc_ref[2 * kv], vp_ref[2 * kv + 1], vc_ref[2 * kv + 1]], axis=1)
        s = s_ref[...] + bias_ref[0]
        es, rdens = [], []
        for half in range(2):
            sink = jnp.zeros((1, nq), F32)
            for p in range(4):
                sink = jnp.where(pair == p, sink_ref[kv * 8 + 2 * p + half] * LOG2E, sink)
            sh = s[half * nk:(half + 1) * nk]
            m = jnp.maximum(jnp.max(sh, axis=0, keepdims=True), sink)
            e = jnp.exp2(sh - m)
            rdens.append(1.0 / (jnp.sum(e, axis=0, keepdims=True) + jnp.exp2(sink - m)))
            es.append(e.astype(BF16))
        o = jnp.dot(vt, jnp.concatenate(es, axis=0), preferred_element_type=F32)
        o = (o * jnp.where(low_rows, rdens[0], rdens[1])).T
        for pr in range(4):
            o_ref[:, (kv * 4 + pr) * LANES:(kv * 4 + pr + 1) * LANES] = o[pr * tq:(pr + 1) * tq].astype(BF16)

    scores(0, sa_ref)
    scores(1, sb_ref)
    attend(0, sa_ref)
    attend(1, sb_ref)


def _swa_prompt(sinks, q, kk, vvt, batch, seq):
    t = q.shape[0]
    tq = SWA_TILE
    nt = seq // tq
    kspec = lambda f: pl.BlockSpec((4, tq, LANES), f)
    vspec = lambda f: pl.BlockSpec((4, LANES, tq), f)
    return pl.pallas_call(
        _swa_prompt_kernel,
        grid=(batch, nt),
        in_specs=[pl.BlockSpec(memory_space=pltpu.SMEM),
                  pl.BlockSpec((tq, C_HEADS * C_DH), lambda b, i: (b * nt + i, 0)),
                  kspec(lambda b, i: (0, b * nt + jnp.maximum(i - 1, 0), 0)), kspec(lambda b, i: (0, b * nt + i, 0)),
                  vspec(lambda b, i: (0, 0, b * nt + jnp.maximum(i - 1, 0))), vspec(lambda b, i: (0, 0, b * nt + i)),
                  pl.BlockSpec((1, 4 * tq, 4 * tq), lambda b, i: (jnp.minimum(i, 1), 0, 0))],
        out_specs=pl.BlockSpec((tq, C_HEADS * C_DH), lambda b, i: (b * nt + i, 0)),
        out_shape=jax.ShapeDtypeStruct((t, C_HEADS * C_DH), BF16),
        scratch_shapes=[pltpu.VMEM((4 * tq, 4 * tq), F32), pltpu.VMEM((4 * tq, 4 * tq), F32)],
        compiler_params=_params(2), name="swa_prompt",
    )(sinks, q, kk, kk, vvt, vvt, _swa_bias(tq))


def _swa_sample_kernel(sink_ref, q_ref, kk_ref, vv_ref, ck_ref, cv_ref, o_ref):
    sq = q_ref.shape[0]
    nc = ck_ref.shape[1]
    lo = lax.broadcasted_iota(jnp.int32, (nc, LANES), 1) < C_DH

    def halves(c, kv):
        own = jnp.where(lo, c, 0.0) if kv == 0 else jnp.where(lo, 0.0, c)
        swapped = pltpu.roll(own, C_DH, 1)
        return ((own, swapped) if kv == 0 else (swapped, own))

    ck = ck_ref[0]
    cv = cv_ref[0]
    for kv in range(C_KV):
        qs = jnp.concatenate([q_ref[:, (kv * 4 + p) * LANES:(kv * 4 + p + 1) * LANES] for p in range(4)], axis=0)
        ckh = halves(ck, kv)
        cvh = halves(cv, kv)
        o = jnp.zeros((4 * sq, LANES), F32)
        for half in range(2):
            ks = jnp.concatenate([ckh[half].astype(BF16), kk_ref[2 * kv + half]], axis=0)
            vs = jnp.concatenate([cvh[half].astype(BF16), vv_ref[2 * kv + half]], axis=0)
            s = lax.dot_general(qs, ks, NT_DIMS, preferred_element_type=F32)
            sink = jnp.concatenate(
                [jnp.full((sq, 1), sink_ref[kv * 8 + 2 * p + half] * LOG2E, F32) for p in range(4)], axis=0)
            m = jnp.maximum(jnp.max(s, axis=-1, keepdims=True), sink)
            e = jnp.exp2(s - m)
            p = e / (jnp.sum(e, axis=-1, keepdims=True) + jnp.exp2(sink - m))
            o = o + jnp.dot(p.astype(BF16), vs, preferred_element_type=F32)
        for pr in range(4):
            o_ref[:, (kv * 4 + pr) * LANES:(kv * 4 + pr + 1) * LANES] = o[pr * sq:(pr + 1) * sq].astype(BF16)


def _swa_sample(sinks, q, kk, vv, cache_k, cache_v):
    nb, nc, _ = cache_k.shape
    t = q.shape[0]
    sq = t // nb
    return pl.pallas_call(
        _swa_sample_kernel,
        grid=(nb,),
        in_specs=[pl.BlockSpec(memory_space=pltpu.SMEM),
                  pl.BlockSpec((sq, C_HEADS * C_DH), lambda b: (b, 0)),
                  pl.BlockSpec((4, sq, LANES), lambda b: (0, b, 0)),
                  pl.BlockSpec((4, sq, LANES), lambda b: (0, b, 0)),
                  pl.BlockSpec((1, nc, LANES), lambda b: (b, 0, 0)),
                  pl.BlockSpec((1, nc, LANES), lambda b: (b, 0, 0))],
        out_specs=pl.BlockSpec((sq, C_HEADS * C_DH), lambda b: (b, 0)),
        out_shape=jax.ShapeDtypeStruct((t, C_HEADS * C_DH), BF16),
        compiler_params=_params(1), name="swa_sample",
    )(sinks, q, kk, vv, cache_k, cache_v)


def _odd_out_kernel(o_ref, x_ref, w_ref, y_ref):
    y_ref[...] = x_ref[...] + jnp.dot(o_ref[...], w_ref[...], preferred_element_type=F32)


def _odd_out(o, x, w, tm):
    t = x.shape[0]
    row = lambda i: (i, 0)
    return pl.pallas_call(
        _odd_out_kernel,
        grid=(t // tm,),
        in_specs=[pl.BlockSpec((tm, o.shape[1]), row), pl.BlockSpec((tm, D_MODEL), row),
                  pl.BlockSpec(w.shape, lambda i: (0, 0))],
        out_specs=pl.BlockSpec((tm, D_MODEL), row),
        out_shape=jax.ShapeDtypeStruct((t, D_MODEL), F32),
        compiler_params=_params(1), name="odd_out",
    )(o, x, w)


def _shift_rows(up, prev, shift, nseq):
    seq_len = up.shape[0] // nseq
    rolled = pltpu.roll(up, shift, 0)
    first = lax.broadcasted_iota(jnp.int32, (SUBLANES, up.shape[1]), 0) < shift
    pieces = []
    for s in range(nseq):
        lo = s * seq_len
        head = jnp.where(first, pltpu.roll(prev[s], shift, 0), pltpu.roll(up[lo:lo + SUBLANES], shift, 0))
        pieces += [head, rolled[lo + SUBLANES:lo + seq_len]]
    return jnp.concatenate(pieces, axis=0)


def _ffn_kernel(x_ref, g_ref, hist_ref, wu_ref, cw_ref, cb_ref, wd_ref, gf_ref,
                y_ref, st_ref, hn_ref, h_ref, carry_ref, *, nseq, final_norm):
    i = pl.program_id(1)
    tm = x_ref.shape[0]
    seq_len = tm // nseq

    @pl.when(i == 0)
    def _():
        carry_ref[...] = hist_ref[0]

    x = x_ref[...]
    hn_ref[...] = _rmsnorm_rows(x, g_ref[...]).astype(BF16)
    for c in range(N_FF_CHUNKS):
        conv = []
        for sl in (slice(c * FF_CHUNK, (c + 1) * FF_CHUNK), slice(D_FF + c * FF_CHUNK, D_FF + (c + 1) * FF_CHUNK)):
            up = jnp.dot(hn_ref[...], wu_ref[0, :, sl], preferred_element_type=F32)
            prev = [carry_ref[s, :, sl] for s in range(nseq)]
            conv.append(cb_ref[0, :, sl] + _shift_rows(up, prev, 2, nseq) * cw_ref[0, 0:1, sl]
                        + _shift_rows(up, prev, 1, nseq) * cw_ref[0, 1:2, sl] + up * cw_ref[0, 2:3, sl])
            for s in range(nseq):
                carry_ref[s, :, sl] = up[(s + 1) * seq_len - SUBLANES:(s + 1) * seq_len]
        gate, val = conv
        h_ref[:, c * FF_CHUNK:(c + 1) * FF_CHUNK] = (gate * jax.nn.sigmoid(gate) * val).astype(BF16)
    y = x + jnp.dot(h_ref[...], wd_ref[0], preferred_element_type=F32)
    if final_norm:
        y = _rmsnorm_rows(y, gf_ref[...])
    y_ref[...] = y
    st_ref[0] = carry_ref[...]


def _ffn(x, g, hist, wu, cw, cb, wd, layer, g_final, nseq, tm, final_norm):
    t = x.shape[0]
    n_outer = hist.shape[0]
    nt = t // n_outer // tm
    row = lambda b, i: (b * nt + i, 0)
    c2 = lambda b, i: (0, 0)
    lay = lambda b, i: (layer, 0, 0)
    st_block = (1,) + hist.shape[1:]
    st_map = lambda b, i: (b, 0, 0, 0)
    return pl.pallas_call(
        functools.partial(_ffn_kernel, nseq=nseq, final_norm=final_norm),
        grid=(n_outer, nt),
        in_specs=[pl.BlockSpec((tm, D_MODEL), row), pl.BlockSpec((1, D_MODEL), c2),
                  pl.BlockSpec(st_block, st_map),
                  pl.BlockSpec((1,) + wu.shape[1:], lay), pl.BlockSpec((1,) + cw.shape[1:], lay),
                  pl.BlockSpec((1,) + cb.shape[1:], lay), pl.BlockSpec((1,) + wd.shape[1:], lay),
                  pl.BlockSpec((1, D_MODEL), c2)],
        out_specs=[pl.BlockSpec((tm, D_MODEL), row), pl.BlockSpec(st_block, st_map)],
        out_shape=[jax.ShapeDtypeStruct((t, D_MODEL), F32), jax.ShapeDtypeStruct(hist.shape, F32)],
        scratch_shapes=[pltpu.VMEM((tm, D_MODEL), BF16),
                        pltpu.VMEM((tm, D_FF), BF16),
                        pltpu.VMEM(hist.shape[1:], F32)],
        compiler_params=_params(2), name="conv_ffn",
    )(x, g, hist, wu, cw, cb, wd, g_final)


def kernel(x_prompt, x_sample, cache_diff_k, cache_diff_v, state_pool, cache_swa_k, cache_swa_v, state_ffn_conv,
           norm_attn, norm_ffn, norm_final, w_in_even, w_out_even, diff_lambda, diff_subln, pool_w, pool_scale,
           w_in_odd, b_in_odd, w_out_odd, sinks, w_up, conv_w, conv_b, w_down):
    bp, sp, d = x_prompt.shape
    bs, ss, _ = x_sample.shape
    depth = norm_attn.shape[0]
    past = cache_diff_k.shape[2]
    hp = x_prompt.reshape(bp * sp, d)
    hs = x_sample.reshape(bs * ss, d)
    tm_s = bs * ss

    tabs_p = _rope_tables(jnp.arange(sp, dtype=jnp.int32))
    tabs_s = tuple(jnp.tile(t, (bs, 1)) for t in _rope_tables(past + jnp.arange(ss, dtype=jnp.int32)))

    wu_b = w_up.astype(BF16)
    wd_b = w_down.astype(BF16)
    cb3 = conv_b[:, None, :]
    conv_pad = jnp.zeros((bs, SUBLANES - (CONV_W - 1), 2 * D_FF), F32)
    zero_conv = jnp.zeros((bp, 1, SUBLANES, 2 * D_FF), F32)
    zero_pool = jnp.zeros((bp, HIST_ROWS, B_WIDTH), F32)
    n_even = cache_diff_k.shape[0]
    cache_kt = jnp.transpose(cache_diff_k, (0, 1, 3, 4, 5, 2)).reshape(n_even, bs, A_WIDTH, past)
    cache_vr = cache_diff_v.reshape(n_even, bs, past * A_HEADS, 2 * A_DH)

    dkp, dvp, plp, skp, svp, fcp = [], [], [], [], [], []
    dks, dvs, pls, sks, svs, fcs = [], [], [], [], [], []
    for i in range(depth):
        j = i // 2
        g_attn = norm_attn[i][None, :]
        if i % 2 == 0:
            lam_init = 0.8 - 0.6 * math.exp(-0.3 * i)
            w_in = w_in_even[j].astype(BF16)
            w_out = w_out_even[j].astype(BF16)
            pw = pool_w[j].astype(BF16)
            ps = pool_scale[j][None, :]
            lam_p = diff_lambda[j]

            q, kt, kz, v, vt, u = _even_in(hp, g_attn, w_in, tabs_p, ROW_TILE, bp, True)
            a = _diff_prompt(q, kz, vt, lam_p, diff_subln[j][:, None], bp, sp, lam_init)
            hp = _even_out(a, u, zero_pool, hp, pw, ps, w_out, bp, ROW_TILE, 0)
            dkp.append(kt)
            dvp.append(v.reshape(bp, sp, A_HEADS, 2 * A_DH))
            plp.append(u.reshape(bp, sp, B_WIDTH)[:, sp - POOL_HIST:])

            q, k, kb, v, vb, u = _even_in(hs, g_attn, w_in, tabs_s, tm_s, 1, False)
            a = _diff_sample(q, kb, vb, cache_kt, cache_vr, j, lam_p, diff_subln[j][None, :], lam_init)
            hist = jnp.concatenate([jnp.zeros((bs, HIST_ROWS - POOL_HIST, B_WIDTH), F32), state_pool[j]], axis=1)
            hs = _even_out(a, u, hist, hs, pw, ps, w_out, bs, ss, POOL_HIST)
            dks.append(k.reshape(bs, ss, A_HEADS, 2, A_DH))
            dvs.append(v.reshape(bs, ss, A_HEADS, 2 * A_DH))
            pls.append(u.reshape(bs, ss, B_WIDTH)[:, ss - POOL_HIST:])
        else:
            w_in = w_in_odd[j].astype(BF16)
            b_in = b_in_odd[j][None, :]
            w_out = w_out_odd[j].astype(BF16)
            sk = sinks[j]

            q, k, v, kk, vvt = _odd_in(hp, g_attn, w_in, b_in, tabs_p, ROW_TILE, bp, C_CACHE, True)
            o = _swa_prompt(sk, q, kk, vvt, bp, sp)
            hp = _odd_out(o, hp, w_out, ROW_TILE)
            skp.append(k.reshape(bp, C_CACHE, C_KV, C_DH))
            svp.append(v.reshape(bp, C_CACHE, C_KV, C_DH))

            q, k, v, kk, vv = _odd_in(hs, g_attn, w_in, b_in, tabs_s, tm_s, 1, tm_s, False)
            o = _swa_sample(sk, q, kk, vv, cache_swa_k[j].reshape(bs, C_CACHE, C_KV * C_DH),
                            cache_swa_v[j].reshape(bs, C_CACHE, C_KV * C_DH))
            hs = _odd_out(o, hs, w_out, tm_s)
            k_all = jnp.concatenate([cache_swa_k[j], k.reshape(bs, ss, C_KV, C_DH)], axis=1)
            v_all = jnp.concatenate([cache_swa_v[j], v.reshape(bs, ss, C_KV, C_DH)], axis=1)
            sks.append(k_all[:, -C_CACHE:])
            svs.append(v_all[:, -C_CACHE:])

        last = i == depth - 1
        g_ffn = norm_ffn[i][None, :]
        g_fin = norm_final[None, :]
        hp, st = _ffn(hp, g_ffn, zero_conv, wu_b, conv_w, cb3, wd_b, i, g_fin, 1, ROW_TILE, last)
        fcp.append(st[:, 0, SUBLANES - (CONV_W - 1):])
        hist = jnp.concatenate([conv_pad, state_ffn_conv[i]], axis=1)[None]
        hs, st = _ffn(hs, g_ffn, hist, wu_b, conv_w, cb3, wd_b, i, g_fin, bs, tm_s, last)
        fcs.append(st[0, :, SUBLANES - (CONV_W - 1):])

    diff_k_prompt = jnp.transpose(jnp.stack(dkp).reshape(n_even, bp, A_HEADS, 2, A_DH, sp), (0, 1, 5, 2, 3, 4))
    return (hp.reshape(bp, sp, d), hs.reshape(bs, ss, d),
            diff_k_prompt, jnp.stack(dvp), jnp.stack(plp), jnp.stack(skp), jnp.stack(svp), jnp.stack(fcp),
            jnp.stack(dks), jnp.stack(dvs), jnp.stack(pls), jnp.stack(sks), jnp.stack(svs), jnp.stack(fcs))
```

```python
import functools
import math

import jax
import jax.numpy as jnp
from jax import lax
from jax.experimental import pallas as pl
from jax.experimental.pallas import tpu as pltpu

F32 = jnp.float32
BF16 = jnp.bfloat16

D_MODEL = 1024
CHUNK = 64
ROPE_THETA = 10000.0
EPS = 1e-5
A_HEADS = 4
A_DH = 64
A_WIDTH = A_HEADS * 2 * A_DH
POOL_WINDOWS = (2, 4, 8, 16)
POOL_CH = 128
POOL_HIST = 15
B_WIDTH = 512
C_HEADS = 16
C_KV = 2
C_DH = 64
C_CACHE = 128
D_FF = 2816
CONV_W = 3

LOG2E = math.log2(math.e)
LANES = 128
SUBLANES = 8
HIST_ROWS = 16
FF_CHUNK = 256
N_FF_CHUNKS = D_FF // FF_CHUNK
ROW_TILE = 512
ATTN_TILE = 512
SWA_TILE = 2 * CHUNK
VMEM_LIMIT = 56 * 1024 * 1024

NT_DIMS = (((1,), (1,)), ((), ()))


def _params(n_axes, vmem=VMEM_LIMIT):
    return pltpu.CompilerParams(dimension_semantics=("arbitrary",) * n_axes, vmem_limit_bytes=vmem)


def _rmsnorm_rows(x, g):
    return x * lax.rsqrt(jnp.mean(x * x, axis=-1, keepdims=True) + EPS) * g


def _rope128(z, cos, s_up, s_dn):
    return z * cos + pltpu.roll(z, 96, 1) * s_up + pltpu.roll(z, 32, 1) * s_dn


def _rope_tables(pos):
    inv = ROPE_THETA ** (-jnp.arange(0, A_DH, 2, dtype=F32) / A_DH)
    ang = pos.astype(F32)[:, None] * inv[None, :]
    cos, sin = jnp.cos(ang), jnp.sin(ang)
    zero = jnp.zeros_like(sin)
    cos128 = jnp.tile(cos, (1, 4))
    s_up = jnp.tile(jnp.concatenate([-sin, zero], axis=1), (1, 2))
    s_dn = jnp.tile(jnp.concatenate([zero, sin], axis=1), (1, 2))
    return cos128, s_up, s_dn


def _even_in_kernel(x_ref, g_ref, w_ref, cos_ref, sup_ref, sdn_ref,
                    q_ref, k_ref, kb_ref, v_ref, vx_ref, u_ref, *, prompt):
    tm = x_ref.shape[0]
    hn = _rmsnorm_rows(x_ref[...], g_ref[...]).astype(BF16)
    z = jnp.dot(hn, w_ref[...], preferred_element_type=F32)
    cos, s_up, s_dn = cos_ref[...], sup_ref[...], sdn_ref[...]
    for c in range(A_HEADS):
        sl = slice(c * LANES, (c + 1) * LANES)
        rq = _rope128(z[:, sl], cos, s_up, s_dn)
        q_ref[:, sl] = (rq * (A_DH ** -0.5 * LOG2E)).astype(BF16)
        rk = _rope128(z[:, A_WIDTH + c * LANES:A_WIDTH + (c + 1) * LANES], cos, s_up, s_dn)
        kb_ref[:, sl] = rk.astype(BF16)
        if prompt:
            k_ref[0, sl, :] = rk.T
        else:
            k_ref[:, sl] = rk
    v = z[:, 2 * A_WIDTH:3 * A_WIDTH]
    v_ref[...] = v
    if prompt:
        vx_ref[0] = v.T.astype(BF16)
    else:
        vx_ref[...] = v.astype(BF16)
    u_ref[...] = z[:, 3 * A_WIDTH:]


def _even_in(x, g, w, tabs, tm, nseq, prompt):
    t = x.shape[0]
    n_tab = tabs[0].shape[0] // tm
    nt = t // nseq // tm
    row = lambda i: (i, 0)
    tab = lambda i: (i % n_tab, 0)
    const = lambda i: (0, 0)
    if prompt:
        k_spec = pl.BlockSpec((1, A_WIDTH, tm), lambda i: (i // nt, 0, i % nt))
        k_shape = jax.ShapeDtypeStruct((nseq, A_WIDTH, t // nseq), F32)
        vx_spec = pl.BlockSpec((1, A_WIDTH, tm), lambda i: (i, 0, 0))
        vx_shape = jax.ShapeDtypeStruct((t // tm, A_WIDTH, tm), BF16)
    else:
        k_spec = pl.BlockSpec((tm, A_WIDTH), row)
        k_shape = jax.ShapeDtypeStruct((t, A_WIDTH), F32)
        vx_spec = pl.BlockSpec((tm, A_WIDTH), row)
        vx_shape = jax.ShapeDtypeStruct((t, A_WIDTH), BF16)
    return pl.pallas_call(
        functools.partial(_even_in_kernel, prompt=prompt),
        grid=(t // tm,),
        in_specs=[pl.BlockSpec((tm, D_MODEL), row), pl.BlockSpec((1, D_MODEL), const),
                  pl.BlockSpec(w.shape, const),
                  pl.BlockSpec((tm, LANES), tab), pl.BlockSpec((tm, LANES), tab), pl.BlockSpec((tm, LANES), tab)],
        out_specs=[pl.BlockSpec((tm, A_WIDTH), row), k_spec, pl.BlockSpec((tm, A_WIDTH), row),
                   pl.BlockSpec((tm, A_WIDTH), row), vx_spec,
                   pl.BlockSpec((tm, B_WIDTH), row)],
        out_shape=[jax.ShapeDtypeStruct((t, A_WIDTH), BF16), k_shape, jax.ShapeDtypeStruct((t, A_WIDTH), BF16),
                   jax.ShapeDtypeStruct((t, A_WIDTH), F32), vx_shape,
                   jax.ShapeDtypeStruct((t, B_WIDTH), F32)],
        compiler_params=_params(1), name="even_in",
    )(x, g, w, *tabs)


def _odd_in_kernel(x_ref, g_ref, w_ref, b_ref, cos_ref, sup_ref, sdn_ref,
                   q_ref, k_ref, v_ref, kk_ref, vv_ref, *, transpose_v):
    tm = x_ref.shape[0]
    hn = _rmsnorm_rows(x_ref[...], g_ref[...]).astype(BF16)
    z = jnp.dot(hn, w_ref[...], preferred_element_type=F32) + b_ref[...]
    cos, s_up, s_dn = cos_ref[...], sup_ref[...], sdn_ref[...]
    nq = C_HEADS * C_DH
    for c in range(nq // LANES):
        sl = slice(c * LANES, (c + 1) * LANES)
        q_ref[:, sl] = (_rope128(z[:, sl], cos, s_up, s_dn) * (C_DH ** -0.5 * LOG2E)).astype(BF16)
    k = _rope128(z[:, nq:nq + LANES], cos, s_up, s_dn)
    v = z[:, nq + LANES:nq + 2 * LANES]
    tail = k_ref.shape[0]
    k_ref[...] = k[tm - tail:]
    v_ref[...] = v[tm - tail:]
    lo = lax.broadcasted_iota(jnp.int32, (tm, LANES), 1) < C_DH
    for src, dst, tr in ((k, kk_ref, False), (v, vv_ref, transpose_v)):
        h0 = jnp.where(lo, src, 0.0)
        h1 = jnp.where(lo, 0.0, src)
        for n, val in enumerate((h0, pltpu.roll(h0, C_DH, 1), pltpu.roll(h1, C_DH, 1), h1)):
            dst[n] = (val.T if tr else val).astype(BF16)


def _odd_in(x, g, w, b, tabs, tm, nseq, tail, transpose_v):
    t = x.shape[0]
    n_tab = tabs[0].shape[0] // tm
    nt = t // nseq // tm
    nq = C_HEADS * C_DH
    row = lambda i: (i, 0)
    kv_spec = pl.BlockSpec((tail, LANES), lambda i: (i // nt, 0))
    kv_shape = jax.ShapeDtypeStruct((nseq * tail, LANES), F32)
    tab = lambda i: (i % n_tab, 0)
    const = lambda i: (0, 0)
    if transpose_v:
        vv_spec = pl.BlockSpec((4, LANES, tm), lambda i: (0, 0, i))
        vv_shape = jax.ShapeDtypeStruct((4, LANES, t), BF16)
    else:
        vv_spec = pl.BlockSpec((4, tm, LANES), lambda i: (0, i, 0))
        vv_shape = jax.ShapeDtypeStruct((4, t, LANES), BF16)
    return pl.pallas_call(
        functools.partial(_odd_in_kernel, transpose_v=transpose_v),
        grid=(t // tm,),
        in_specs=[pl.BlockSpec((tm, D_MODEL), row), pl.BlockSpec((1, D_MODEL), const),
                  pl.BlockSpec(w.shape, const), pl.BlockSpec((1, w.shape[1]), const),
                  pl.BlockSpec((tm, LANES), tab), pl.BlockSpec((tm, LANES), tab), pl.BlockSpec((tm, LANES), tab)],
        out_specs=[pl.BlockSpec((tm, nq), row), kv_spec, kv_spec,
                   pl.BlockSpec((4, tm, LANES), lambda i: (0, i, 0)), vv_spec],
        out_shape=[jax.ShapeDtypeStruct((t, nq), BF16), kv_shape, kv_shape,
                   jax.ShapeDtypeStruct((4, t, LANES), BF16), vv_shape],
        compiler_params=_params(1), name="odd_in",
    )(x, g, w, b, *tabs)


def _diff_lambda(lam_ref, lam_init):
    lp = lam_ref[...]
    return (jnp.exp(jnp.sum(lp[0:1] * lp[1:2], axis=-1, keepdims=True))
            - jnp.exp(jnp.sum(lp[2:3] * lp[3:4], axis=-1, keepdims=True)) + lam_init)


def _diff_prompt_kernel(q_ref, kb_ref, vt_ref, lam_ref, sub_ref, o_ref, m_ref, l_ref, acc_ref, sa_ref, sb_ref,
                        *, lam_init):
    i = pl.program_id(2)
    tq = q_ref.shape[0]
    tk = tq
    q = q_ref[...].astype(F32)
    lo = lax.broadcasted_iota(jnp.int32, (tq, LANES), 1) < A_DH
    qm = (jnp.where(lo, q, 0.0).astype(BF16), jnp.where(lo, 0.0, q).astype(BF16))
    m_ref[...] = jnp.full(m_ref.shape, -jnp.inf, F32)
    l_ref[...] = jnp.zeros(l_ref.shape, F32)
    acc_ref[...] = jnp.zeros(acc_ref.shape, F32)

    def scores(j, s_ref):
        start = pl.multiple_of(j * tk, tk)
        k = kb_ref[pl.ds(start, tk), :]
        for mp in range(2):
            s_ref[mp] = lax.dot_general(k, qm[mp], NT_DIMS, preferred_element_type=F32)

    def consume(j, s_ref, mask):
        vt = vt_ref[j]
        for mp in range(2):
            s = s_ref[mp]
            if mask is not None:
                s = jnp.where(mask, s, -jnp.inf)
            m_prev = m_ref[mp]
            m_new = jnp.maximum(m_prev, jnp.max(s, axis=0, keepdims=True))
            alpha = jnp.exp2(m_prev - m_new)
            p = jnp.exp2(s - m_new)
            l_ref[mp] = alpha * l_ref[mp] + jnp.sum(p, axis=0, keepdims=True)
            acc_ref[mp] = alpha * acc_ref[mp] + jnp.dot(vt, p.astype(BF16), preferred_element_type=F32)
            m_ref[mp] = m_new

    scores(0, sa_ref)

    def pair(j0):
        scores(j0 + 1, sb_ref)
        consume(j0, sa_ref, None)
        scores(j0 + 2, sa_ref)
        consume(j0 + 1, sb_ref, None)

    def quad(t, carry):
        pair(4 * t)
        pair(4 * t + 2)
        return carry

    n_quads = lax.shift_right_logical(i, 2)
    lax.fori_loop(0, n_quads, quad, 0)

    @pl.when((i & 2) == 2)
    def _():
        pair(4 * n_quads)

    k_chunk = lax.broadcasted_iota(jnp.int32, (tk, tq), 0) // CHUNK
    q_chunk = lax.broadcasted_iota(jnp.int32, (tk, tq), 1) // CHUNK
    diag = k_chunk <= q_chunk
    odd = (i & 1) == 1

    @pl.when(odd)
    def _():
        scores(i, sb_ref)
        consume(i - 1, sa_ref, None)
        consume(i, sb_ref, diag)

    @pl.when(jnp.logical_not(odd))
    def _():
        consume(i, sa_ref, diag)

    lam = _diff_lambda(lam_ref, lam_init)
    a = acc_ref[0] / l_ref[0] - lam * (acc_ref[1] / l_ref[1])
    a = a * lax.rsqrt(jnp.mean(a * a, axis=0, keepdims=True) + EPS) * sub_ref[...] * (1.0 - lam_init)
    o_ref[...] = a.T.astype(BF16)


def _diff_prompt(q, kb, vt, lam_p, sub_col, batch, seq, lam_init):
    t = q.shape[0]
    tq = ATTN_TILE
    nq = seq // tq
    assert vt.shape == (t // tq, A_WIDTH, tq)
    return pl.pallas_call(
        functools.partial(_diff_prompt_kernel, lam_init=lam_init),
        grid=(batch, A_HEADS, nq),
        in_specs=[pl.BlockSpec((tq, LANES), lambda b, h, i: (b * nq + i, h)),
                  pl.BlockSpec((seq, LANES), lambda b, h, i: (b, h)),
                  pl.BlockSpec((nq, LANES, tq), lambda b, h, i: (b, h, 0)),
                  pl.BlockSpec((4, A_DH), lambda b, h, i: (0, 0)),
                  pl.BlockSpec((LANES, 1), lambda b, h, i: (0, 0))],
        out_specs=pl.BlockSpec((tq, LANES), lambda b, h, i: (b * nq + i, h)),
        out_shape=jax.ShapeDtypeStruct((t, A_WIDTH), BF16),
        scratch_shapes=[pltpu.VMEM((2, 1, tq), F32), pltpu.VMEM((2, 1, tq), F32),
                        pltpu.VMEM((2, LANES, tq), F32),
                        pltpu.VMEM((2, tq, tq), F32), pltpu.VMEM((2, tq, tq), F32)],
        compiler_params=_params(3), name="diff_attn_prompt",
    )(q, kb, vt, lam_p, sub_col)


def _diff_sample_kernel(q_ref, kn_ref, vn_ref, ckt_ref, cv_ref, lam_ref, sub_ref, o_ref, *, lam_init):
    sq = q_ref.shape[0]
    past = ckt_ref.shape[3]
    lo = lax.broadcasted_iota(jnp.int32, (sq, LANES), 1) < A_DH
    lam = _diff_lambda(lam_ref, lam_init)
    for h in range(A_HEADS):
        sl = slice(h * LANES, (h + 1) * LANES)
        q = q_ref[:, sl].astype(F32)
        q2 = jnp.concatenate([jnp.where(lo, q, 0.0), jnp.where(lo, 0.0, q)], axis=0).astype(BF16)
        s_c = jnp.dot(q2, ckt_ref[0, 0, sl, :].astype(BF16), preferred_element_type=F32)
        s_n = lax.dot_general(q2, kn_ref[:, sl], NT_DIMS, preferred_element_type=F32)
        m = jnp.maximum(jnp.max(s_c, axis=-1, keepdims=True), jnp.max(s_n, axis=-1, keepdims=True))
        e_c = jnp.exp2(s_c - m)
        e_n = jnp.exp2(s_n - m)
        den = jnp.sum(e_c, axis=-1, keepdims=True) + jnp.sum(e_n, axis=-1, keepdims=True)
        cv = cv_ref[0, 0, pl.ds(h, past, stride=A_HEADS), :].astype(BF16)
        o = (jnp.dot(e_c.astype(BF16), cv, preferred_element_type=F32)
             + jnp.dot(e_n.astype(BF16), vn_ref[:, sl], preferred_element_type=F32)) / den
        a = o[:sq] - lam * o[sq:]
        o_ref[:, sl] = (_rmsnorm_rows(a, sub_ref[...]) * (1.0 - lam_init)).astype(BF16)


def _diff_sample(q, kn, vn, cache_kt, cache_v, layer, lam_p, sub, lam_init):
    _, nb, _, past = cache_kt.shape
    t = q.shape[0]
    sq = t // nb
    row = lambda b: (b, 0)
    return pl.pallas_call(
        functools.partial(_diff_sample_kernel, lam_init=lam_init),
        grid=(nb,),
        in_specs=[pl.BlockSpec((sq, A_WIDTH), row), pl.BlockSpec((sq, A_WIDTH), row), pl.BlockSpec((sq, A_WIDTH), row),
                  pl.BlockSpec((1, 1, A_WIDTH, past), lambda b: (layer, b, 0, 0)),
                  pl.BlockSpec((1, 1, past * A_HEADS, LANES), lambda b: (layer, b, 0, 0)),
                  pl.BlockSpec((4, A_DH), lambda b: (0, 0)),
                  pl.BlockSpec((1, LANES), lambda b: (0, 0))],
        out_specs=pl.BlockSpec((sq, A_WIDTH), row),
        out_shape=jax.ShapeDtypeStruct((t, A_WIDTH), BF16),
        compiler_params=_params(1), name="diff_attn_sample",
    )(q, kn, vn, cache_kt, cache_v, lam_p, sub)


def _pool_kernel(a_ref, u_ref, hist_ref, pw_ref, ps_ref, o_ref, ext_ref, *, n_hist):
    i = pl.program_id(1)
    tm = u_ref.shape[0]

    @pl.when(i == 0)
    def _():
        ext_ref[0:HIST_ROWS, :] = hist_ref[0]

    u = u_ref[...]
    ext_ref[HIST_ROWS:HIST_ROWS + tm, :] = u
    pos = (i * tm + lax.broadcasted_iota(jnp.int32, (tm, 1), 0)).astype(F32)
    o_ref[:, 0:A_WIDTH] = a_ref[...]
    for g, win in enumerate(POOL_WINDOWS):
        sl = slice(g * POOL_CH, (g + 1) * POOL_CH)
        tot = u[:, sl]
        for back in range(1, win):
            tot = tot + ext_ref[HIST_ROWS - back:HIST_ROWS - back + tm, sl]
        cnt = jnp.minimum(float(win), pos + (1.0 + n_hist))
        pooled = (tot / cnt - u[:, sl]).astype(BF16)
        y = jnp.dot(pooled, pw_ref[g], preferred_element_type=F32) * ps_ref[:, sl]
        o_ref[:, A_WIDTH + g * POOL_CH:A_WIDTH + (g + 1) * POOL_CH] = y.astype(BF16)
    ext_ref[0:HIST_ROWS, :] = ext_ref[tm:tm + HIST_ROWS, :]


def _pool(a, u, hist, pool_w, pool_scale, nseq, tm, n_hist):
    t = a.shape[0]
    nt = t // nseq // tm
    row = lambda b, i: (b * nt + i, 0)
    return pl.pallas_call(
        functools.partial(_pool_kernel, n_hist=n_hist),
        grid=(nseq, nt),
        in_specs=[pl.BlockSpec((tm, A_WIDTH), row), pl.BlockSpec((tm, B_WIDTH), row),
                  pl.BlockSpec((1, HIST_ROWS, B_WIDTH), lambda b, i: (b, 0, 0)),
                  pl.BlockSpec(pool_w.shape, lambda b, i: (0, 0, 0)),
                  pl.BlockSpec((1, B_WIDTH), lambda b, i: (0, 0))],
        out_specs=pl.BlockSpec((tm, D_MODEL), row),
        out_shape=jax.ShapeDtypeStruct((t, D_MODEL), BF16),
        scratch_shapes=[pltpu.VMEM((HIST_ROWS + tm, B_WIDTH), F32)],
        compiler_params=_params(2), name="pool_mix",
    )(a, u, hist, pool_w, pool_scale)


def _swa_bias(tq):
    nk, nq = 2 * tq, 4 * tq
    kc = (jnp.arange(2 * nk) % nk // CHUNK)[:, None]
    qc = (jnp.arange(nq) % tq // CHUNK)[None, :]
    band = (kc >= qc) & (kc <= qc + 2)
    return jnp.where(jnp.stack([band & (kc >= 2), band]), 0.0, -jnp.inf).astype(F32)


def _swa_prompt_kernel(sink_ref, q_ref, kp_ref, kc_ref, vp_ref, vc_ref, bias_ref, o_ref, sa_ref, sb_ref):
    tq = q_ref.shape[0]
    nk = 2 * tq
    nq = 4 * tq
    pair = lax.broadcasted_iota(jnp.int32, (1, nq), 1) // tq
    low_rows = lax.broadcasted_iota(jnp.int32, (LANES, 1), 0) < C_DH

    def scores(kv, s_ref):
        qs = jnp.concatenate([q_ref[:, (kv * 4 + p) * LANES:(kv * 4 + p + 1) * LANES] for p in range(4)], axis=0)
        ks = jnp.concatenate([kp_ref[2 * kv], kc_ref[2 * kv], kp_ref[2 * kv + 1], kc_ref[2 * kv + 1]], axis=0)
        s_ref[...] = lax.dot_general(ks, qs, NT_DIMS, preferred_element_type=F32)

    def attend(kv, s_ref):
        vt = jnp.concatenate([vp_ref[2 * kv], vc_ref[2 * kv], vp_ref[2 * kv + 1], vc_ref[2 * kv + 1]], axis=1)
        s = s_ref[...] + bias_ref[0]
        es, rdens = [], []
        for half in range(2):
            sink = jnp.zeros((1, nq), F32)
            for p in range(4):
                sink = jnp.where(pair == p, sink_ref[kv * 8 + 2 * p + half] * LOG2E, sink)
            sh = s[half * nk:(half + 1) * nk]
            m = jnp.maximum(jnp.max(sh, axis=0, keepdims=True), sink)
            e = jnp.exp2(sh - m)
            rdens.append(1.0 / (jnp.sum(e, axis=0, keepdims=True) + jnp.exp2(sink - m)))
            es.append(e.astype(BF16))
        o = jnp.dot(vt, jnp.concatenate(es, axis=0), preferred_element_type=F32)
        o = (o * jnp.where(low_rows, rdens[0], rdens[1])).T
        for pr in range(4):
            o_ref[:, (kv * 4 + pr) * LANES:(kv * 4 + pr + 1) * LANES] = o[pr * tq:(pr + 1) * tq].astype(BF16)

    scores(0, sa_ref)
    scores(1, sb_ref)
    attend(0, sa_ref)
    attend(1, sb_ref)


def _swa_prompt(sinks, q, kk, vvt, batch, seq):
    t = q.shape[0]
    tq = SWA_TILE
    nt = seq // tq
    kspec = lambda f: pl.BlockSpec((4, tq, LANES), f)
    vspec = lambda f: pl.BlockSpec((4, LANES, tq), f)
    return pl.pallas_call(
        _swa_prompt_kernel,
        grid=(batch, nt),
        in_specs=[pl.BlockSpec(memory_space=pltpu.SMEM),
                  pl.BlockSpec((tq, C_HEADS * C_DH), lambda b, i: (b * nt + i, 0)),
                  kspec(lambda b, i: (0, b * nt + jnp.maximum(i - 1, 0), 0)), kspec(lambda b, i: (0, b * nt + i, 0)),
                  vspec(lambda b, i: (0, 0, b * nt + jnp.maximum(i - 1, 0))), vspec(lambda b, i: (0, 0, b * nt + i)),
                  pl.BlockSpec((1, 4 * tq, 4 * tq), lambda b, i: (jnp.minimum(i, 1), 0, 0))],
        out_specs=pl.BlockSpec((tq, C_HEADS * C_DH), lambda b, i: (b * nt + i, 0)),
        out_shape=jax.ShapeDtypeStruct((t, C_HEADS * C_DH), BF16),
        scratch_shapes=[pltpu.VMEM((4 * tq, 4 * tq), F32), pltpu.VMEM((4 * tq, 4 * tq), F32)],
        compiler_params=_params(2), name="swa_prompt",
    )(sinks, q, kk, kk, vvt, vvt, _swa_bias(tq))


def _swa_sample_kernel(sink_ref, q_ref, kk_ref, vv_ref, ck_ref, cv_ref, o_ref):
    sq = q_ref.shape[0]
    nc = ck_ref.shape[1]
    lo = lax.broadcasted_iota(jnp.int32, (nc, LANES), 1) < C_DH

    def halves(c, kv):
        own = jnp.where(lo, c, 0.0) if kv == 0 else jnp.where(lo, 0.0, c)
        swapped = pltpu.roll(own, C_DH, 1)
        return ((own, swapped) if kv == 0 else (swapped, own))

    ck = ck_ref[0]
    cv = cv_ref[0]
    for kv in range(C_KV):
        qs = jnp.concatenate([q_ref[:, (kv * 4 + p) * LANES:(kv * 4 + p + 1) * LANES] for p in range(4)], axis=0)
        ckh = halves(ck, kv)
        cvh = halves(cv, kv)
        o = jnp.zeros((4 * sq, LANES), F32)
        for half in range(2):
            ks = jnp.concatenate([ckh[half].astype(BF16), kk_ref[2 * kv + half]], axis=0)
            vs = jnp.concatenate([cvh[half].astype(BF16), vv_ref[2 * kv + half]], axis=0)
            s = lax.dot_general(qs, ks, NT_DIMS, preferred_element_type=F32)
            sink = jnp.concatenate(
                [jnp.full((sq, 1), sink_ref[kv * 8 + 2 * p + half] * LOG2E, F32) for p in range(4)], axis=0)
            m = jnp.maximum(jnp.max(s, axis=-1, keepdims=True), sink)
            e = jnp.exp2(s - m)
            p = e / (jnp.sum(e, axis=-1, keepdims=True) + jnp.exp2(sink - m))
            o = o + jnp.dot(p.astype(BF16), vs, preferred_element_type=F32)
        for pr in range(4):
            o_ref[:, (kv * 4 + pr) * LANES:(kv * 4 + pr + 1) * LANES] = o[pr * sq:(pr + 1) * sq].astype(BF16)


def _swa_sample(sinks, q, kk, vv, cache_k, cache_v):
    nb, nc, _ = cache_k.shape
    t = q.shape[0]
    sq = t // nb
    return pl.pallas_call(
        _swa_sample_kernel,
        grid=(nb,),
        in_specs=[pl.BlockSpec(memory_space=pltpu.SMEM),
                  pl.BlockSpec((sq, C_HEADS * C_DH), lambda b: (b, 0)),
                  pl.BlockSpec((4, sq, LANES), lambda b: (0, b, 0)),
                  pl.BlockSpec((4, sq, LANES), lambda b: (0, b, 0)),
                  pl.BlockSpec((1, nc, LANES), lambda b: (b, 0, 0)),
                  pl.BlockSpec((1, nc, LANES), lambda b: (b, 0, 0))],
        out_specs=pl.BlockSpec((sq, C_HEADS * C_DH), lambda b: (b, 0)),
        out_shape=jax.ShapeDtypeStruct((t, C_HEADS * C_DH), BF16),
        compiler_params=_params(1), name="swa_sample",
    )(sinks, q, kk, vv, cache_k, cache_v)


def _shift_rows(up, prev, shift, nseq):
    seq_len = up.shape[0] // nseq
    rolled = pltpu.roll(up, shift, 0)
    first = lax.broadcasted_iota(jnp.int32, (SUBLANES, up.shape[1]), 0) < shift
    pieces = []
    for s in range(nseq):
        lo = s * seq_len
        head = jnp.where(first, pltpu.roll(prev[s], shift, 0), pltpu.roll(up[lo:lo + SUBLANES], shift, 0))
        pieces += [head, rolled[lo + SUBLANES:lo + seq_len]]
    return jnp.concatenate(pieces, axis=0)


def _ffn_kernel(x_ref, mix_ref, wo_ref, g_ref, hist_ref, wu_ref, cw_ref, cb_ref, wd_ref, gf_ref,
                y_ref, st_ref, hn_ref, h_ref, carry_ref, *, nseq, final_norm):
    i = pl.program_id(1)
    tm = x_ref.shape[0]
    seq_len = tm // nseq

    @pl.when(i == 0)
    def _():
        carry_ref[...] = hist_ref[0]

    x = x_ref[...] + jnp.dot(mix_ref[...], wo_ref[0], preferred_element_type=F32)
    hn_ref[...] = _rmsnorm_rows(x, g_ref[...]).astype(BF16)
    for c in range(N_FF_CHUNKS):
        conv = []
        for sl in (slice(c * FF_CHUNK, (c + 1) * FF_CHUNK), slice(D_FF + c * FF_CHUNK, D_FF + (c + 1) * FF_CHUNK)):
            up = jnp.dot(hn_ref[...], wu_ref[0, :, sl], preferred_element_type=F32)
            prev = [carry_ref[s, :, sl] for s in range(nseq)]
            conv.append(cb_ref[0, :, sl] + _shift_rows(up, prev, 2, nseq) * cw_ref[0, 0:1, sl]
                        + _shift_rows(up, prev, 1, nseq) * cw_ref[0, 1:2, sl] + up * cw_ref[0, 2:3, sl])
            for s in range(nseq):
                carry_ref[s, :, sl] = up[(s + 1) * seq_len - SUBLANES:(s + 1) * seq_len]
        gate, val = conv
        h_ref[:, c * FF_CHUNK:(c + 1) * FF_CHUNK] = (gate * jax.nn.sigmoid(gate) * val).astype(BF16)
    y = x + jnp.dot(h_ref[...], wd_ref[0], preferred_element_type=F32)
    if final_norm:
        y = _rmsnorm_rows(y, gf_ref[...])
    y_ref[...] = y
    st_ref[0] = carry_ref[...]


def _ffn(x, mix, wo, mix_layer, g, hist, wu, cw, cb, wd, layer, g_final, nseq, tm, final_norm):
    t = x.shape[0]
    n_outer = hist.shape[0]
    nt = t // n_outer // tm
    row = lambda b, i: (b * nt + i, 0)
    c2 = lambda b, i: (0, 0)
    lay = lambda b, i: (layer, 0, 0)
    once = pl.Buffered(1)
    st_block = (1,) + hist.shape[1:]
    st_map = lambda b, i: (b, 0, 0, 0)
    return pl.pallas_call(
        functools.partial(_ffn_kernel, nseq=nseq, final_norm=final_norm),
        grid=(n_outer, nt),
        in_specs=[pl.BlockSpec((tm, D_MODEL), row), pl.BlockSpec((tm, D_MODEL), row),
                  pl.BlockSpec((1,) + wo.shape[1:], lambda b, i: (mix_layer, 0, 0), pipeline_mode=once),
                  pl.BlockSpec((1, D_MODEL), c2),
                  pl.BlockSpec(st_block, st_map),
                  pl.BlockSpec((1,) + wu.shape[1:], lay, pipeline_mode=once), pl.BlockSpec((1,) + cw.shape[1:], lay),
                  pl.BlockSpec((1,) + cb.shape[1:], lay), pl.BlockSpec((1,) + wd.shape[1:], lay, pipeline_mode=once),
                  pl.BlockSpec((1, D_MODEL), c2)],
        out_specs=[pl.BlockSpec((tm, D_MODEL), row), pl.BlockSpec(st_block, st_map)],
        out_shape=[jax.ShapeDtypeStruct((t, D_MODEL), F32), jax.ShapeDtypeStruct(hist.shape, F32)],
        scratch_shapes=[pltpu.VMEM((tm, D_MODEL), BF16),
                        pltpu.VMEM((tm, D_FF), BF16),
                        pltpu.VMEM(hist.shape[1:], F32)],
        compiler_params=_params(2), name="conv_ffn",
    )(x, mix, wo, g, hist, wu, cw, cb, wd, g_final)


def kernel(x_prompt, x_sample, cache_diff_k, cache_diff_v, state_pool, cache_swa_k, cache_swa_v, state_ffn_conv,
           norm_attn, norm_ffn, norm_final, w_in_even, w_out_even, diff_lambda, diff_subln, pool_w, pool_scale,
           w_in_odd, b_in_odd, w_out_odd, sinks, w_up, conv_w, conv_b, w_down):
    bp, sp, d = x_prompt.shape
    bs, ss, _ = x_sample.shape
    depth = norm_attn.shape[0]
    past = cache_diff_k.shape[2]
    hp = x_prompt.reshape(bp * sp, d)
    hs = x_sample.reshape(bs * ss, d)
    tm_s = bs * ss

    tabs_p = _rope_tables(jnp.arange(sp, dtype=jnp.int32))
    tabs_s = tuple(jnp.tile(t, (bs, 1)) for t in _rope_tables(past + jnp.arange(ss, dtype=jnp.int32)))

    wu_b = w_up.astype(BF16)
    wo_even = w_out_even.astype(BF16)
    wo_odd = w_out_odd.astype(BF16)
    wd_b = w_down.astype(BF16)
    cb3 = conv_b[:, None, :]
    conv_pad = jnp.zeros((bs, SUBLANES - (CONV_W - 1), 2 * D_FF), F32)
    zero_conv = jnp.zeros((bp, 1, SUBLANES, 2 * D_FF), F32)
    zero_pool = jnp.zeros((bp, HIST_ROWS, B_WIDTH), F32)
    n_even = cache_diff_k.shape[0]
    cache_kt = jnp.transpose(cache_diff_k, (0, 1, 3, 4, 5, 2)).reshape(n_even, bs, A_WIDTH, past)
    cache_vr = cache_diff_v.reshape(n_even, bs, past * A_HEADS, 2 * A_DH)

    dkp, dvp, plp, skp, svp, fcp = [], [], [], [], [], []
    dks, dvs, pls, sks, svs, fcs = [], [], [], [], [], []
    for i in range(depth):
        j = i // 2
        g_attn = norm_attn[i][None, :]
        if i % 2 == 0:
            lam_init = 0.8 - 0.6 * math.exp(-0.3 * i)
            w_in = w_in_even[j].astype(BF16)
            pw = pool_w[j].astype(BF16)
            ps = pool_scale[j][None, :]
            lam_p = diff_lambda[j]
            wo = wo_even

            q, kt, kb, v, vt, u = _even_in(hp, g_attn, w_in, tabs_p, ROW_TILE, bp, True)
            a = _diff_prompt(q, kb, vt, lam_p, diff_subln[j][:, None], bp, sp, lam_init)
            mix_p = _pool(a, u, zero_pool, pw, ps, bp, ROW_TILE, 0)
            dkp.append(kt)
            dvp.append(v.reshape(bp, sp, A_HEADS, 2 * A_DH))
            plp.append(u.reshape(bp, sp, B_WIDTH)[:, sp - POOL_HIST:])

            q, k, kb, v, vb, u = _even_in(hs, g_attn, w_in, tabs_s, tm_s, 1, False)
            a = _diff_sample(q, kb, vb, cache_kt, cache_vr, j, lam_p, diff_subln[j][None, :], lam_init)
            hist = jnp.concatenate([jnp.zeros((bs, HIST_ROWS - POOL_HIST, B_WIDTH), F32), state_pool[j]], axis=1)
            mix_s = _pool(a, u, hist, pw, ps, bs, ss, POOL_HIST)
            dks.append(k.reshape(bs, ss, A_HEADS, 2, A_DH))
            dvs.append(v.reshape(bs, ss, A_HEADS, 2 * A_DH))
            pls.append(u.reshape(bs, ss, B_WIDTH)[:, ss - POOL_HIST:])
        else:
            w_in = w_in_odd[j].astype(BF16)
            b_in = b_in_odd[j][None, :]
            sk = sinks[j]
            wo = wo_odd

            q, k, v, kk, vvt = _odd_in(hp, g_attn, w_in, b_in, tabs_p, ROW_TILE, bp, C_CACHE, True)
            mix_p = _swa_prompt(sk, q, kk, vvt, bp, sp)
            skp.append(k.reshape(bp, C_CACHE, C_KV, C_DH))
            svp.append(v.reshape(bp, C_CACHE, C_KV, C_DH))

            q, k, v, kk, vv = _odd_in(hs, g_attn, w_in, b_in, tabs_s, tm_s, 1, tm_s, False)
            mix_s = _swa_sample(sk, q, kk, vv, cache_swa_k[j].reshape(bs, C_CACHE, C_KV * C_DH),
                                cache_swa_v[j].reshape(bs, C_CACHE, C_KV * C_DH))
            k_all = jnp.concatenate([cache_swa_k[j], k.reshape(bs, ss, C_KV, C_DH)], axis=1)
            v_all = jnp.concatenate([cache_swa_v[j], v.reshape(bs, ss, C_KV, C_DH)], axis=1)
            sks.append(k_all[:, -C_CACHE:])
            svs.append(v_all[:, -C_CACHE:])

        last = i == depth - 1
        g_ffn = norm_ffn[i][None, :]
        g_fin = norm_final[None, :]
        hp, st = _ffn(hp, mix_p, wo, j, g_ffn, zero_conv, wu_b, conv_w, cb3, wd_b, i, g_fin, 1, ROW_TILE, last)
        fcp.append(st[:, 0, SUBLANES - (CONV_W - 1):])
        hist = jnp.concatenate([conv_pad, state_ffn_conv[i]], axis=1)[None]
        hs, st = _ffn(hs, mix_s, wo, j, g_ffn, hist, wu_b, conv_w, cb3, wd_b, i, g_fin, bs, tm_s, last)
        fcs.append(st[0, :, SUBLANES - (CONV_W - 1):])

    diff_k_prompt = jnp.transpose(jnp.stack(dkp).reshape(n_even, bp, A_HEADS, 2, A_DH, sp), (0, 1, 5, 2, 3, 4))
    return (hp.reshape(bp, sp, d), hs.reshape(bs, ss, d),
            diff_k_prompt, jnp.stack(dvp), jnp.stack(plp), jnp.stack(skp), jnp.stack(svp), jnp.stack(fcp),
            jnp.stack(dks), jnp.stack(dvs), jnp.stack(pls), jnp.stack(sks), jnp.stack(svs), jnp.stack(fcs))
```

```python
import functools
import math

import jax
import jax.numpy as jnp
from jax import lax
from jax.experimental import pallas as pl
from jax.experimental.pallas import tpu as pltpu

F32 = jnp.float32
BF16 = jnp.bfloat16

D_MODEL = 1024
CHUNK = 64
ROPE_THETA = 10000.0
EPS = 1e-5
A_HEADS = 4
A_DH = 64
A_WIDTH = A_HEADS * 2 * A_DH
POOL_WINDOWS = (2, 4, 8, 16)
POOL_CH = 128
POOL_HIST = 15
B_WIDTH = 512
C_HEADS = 16
C_KV = 2
C_DH = 64
C_CACHE = 128
D_FF = 2816
CONV_W = 3

LOG2E = math.log2(math.e)
LANES = 128
SUBLANES = 8
HIST_ROWS = 16
FF_CHUNK = 256
N_FF_CHUNKS = D_FF // FF_CHUNK
ROW_TILE = 512
ATTN_TILE = 512
SWA_TILE = 2 * CHUNK
VMEM_LIMIT = 56 * 1024 * 1024

NT_DIMS = (((1,), (1,)), ((), ()))


def _params(n_axes, vmem=VMEM_LIMIT):
    return pltpu.CompilerParams(dimension_semantics=("arbitrary",) * n_axes, vmem_limit_bytes=vmem)


def _rmsnorm_rows(x, g):
    return x * lax.rsqrt(jnp.mean(x * x, axis=-1, keepdims=True) + EPS) * g


def _rope128(z, cos, s_up, s_dn):
    return z * cos + pltpu.roll(z, 96, 1) * s_up + pltpu.roll(z, 32, 1) * s_dn


def _rope_tables(pos):
    inv = ROPE_THETA ** (-jnp.arange(0, A_DH, 2, dtype=F32) / A_DH)
    ang = pos.astype(F32)[:, None] * inv[None, :]
    cos, sin = jnp.cos(ang), jnp.sin(ang)
    zero = jnp.zeros_like(sin)
    cos128 = jnp.tile(cos, (1, 4))
    s_up = jnp.tile(jnp.concatenate([-sin, zero], axis=1), (1, 2))
    s_dn = jnp.tile(jnp.concatenate([zero, sin], axis=1), (1, 2))
    return cos128, s_up, s_dn


def _even_in_kernel(x_ref, g_ref, w_ref, cos_ref, sup_ref, sdn_ref,
                    q_ref, k_ref, kb_ref, v_ref, vx_ref, u_ref, *, prompt):
    tm = x_ref.shape[0]
    hn = _rmsnorm_rows(x_ref[...], g_ref[...]).astype(BF16)
    z = jnp.dot(hn, w_ref[...], preferred_element_type=F32)
    cos, s_up, s_dn = cos_ref[...], sup_ref[...], sdn_ref[...]
    for c in range(A_HEADS):
        sl = slice(c * LANES, (c + 1) * LANES)
        rq = _rope128(z[:, sl], cos, s_up, s_dn)
        q_ref[:, sl] = (rq * (A_DH ** -0.5 * LOG2E)).astype(BF16)
        rk = _rope128(z[:, A_WIDTH + c * LANES:A_WIDTH + (c + 1) * LANES], cos, s_up, s_dn)
        kb_ref[:, sl] = rk.astype(BF16)
        if prompt:
            k_ref[0, sl, :] = rk.T
        else:
            k_ref[:, sl] = rk
    v = z[:, 2 * A_WIDTH:3 * A_WIDTH]
    if prompt:
        for c in range(A_HEADS):
            v_ref[pl.ds(c, tm, stride=A_HEADS), :] = v[:, c * LANES:(c + 1) * LANES]
        vx_ref[0] = v.T.astype(BF16)
    else:
        v_ref[...] = v
        vx_ref[...] = v.astype(BF16)
    u_ref[...] = z[:, 3 * A_WIDTH:]


def _even_in(x, g, w, tabs, tm, nseq, prompt):
    t = x.shape[0]
    n_tab = tabs[0].shape[0] // tm
    nt = t // nseq // tm
    row = lambda i: (i, 0)
    tab = lambda i: (i % n_tab, 0)
    const = lambda i: (0, 0)
    if prompt:
        k_spec = pl.BlockSpec((1, A_WIDTH, tm), lambda i: (i // nt, 0, i % nt))
        k_shape = jax.ShapeDtypeStruct((nseq, A_WIDTH, t // nseq), F32)
        v_spec = pl.BlockSpec((tm * A_HEADS, LANES), row)
        v_shape = jax.ShapeDtypeStruct((t * A_HEADS, LANES), F32)
        vx_spec = pl.BlockSpec((1, A_WIDTH, tm), lambda i: (i, 0, 0))
        vx_shape = jax.ShapeDtypeStruct((t // tm, A_WIDTH, tm), BF16)
    else:
        k_spec = pl.BlockSpec((tm, A_WIDTH), row)
        k_shape = jax.ShapeDtypeStruct((t, A_WIDTH), F32)
        v_spec = pl.BlockSpec((tm, A_WIDTH), row)
        v_shape = jax.ShapeDtypeStruct((t, A_WIDTH), F32)
        vx_spec = pl.BlockSpec((tm, A_WIDTH), row)
        vx_shape = jax.ShapeDtypeStruct((t, A_WIDTH), BF16)
    return pl.pallas_call(
        functools.partial(_even_in_kernel, prompt=prompt),
        grid=(t // tm,),
        in_specs=[pl.BlockSpec((tm, D_MODEL), row), pl.BlockSpec((1, D_MODEL), const),
                  pl.BlockSpec(w.shape, const),
                  pl.BlockSpec((tm, LANES), tab), pl.BlockSpec((tm, LANES), tab), pl.BlockSpec((tm, LANES), tab)],
        out_specs=[pl.BlockSpec((tm, A_WIDTH), row), k_spec, pl.BlockSpec((tm, A_WIDTH), row),
                   v_spec, vx_spec,
                   pl.BlockSpec((tm, B_WIDTH), row)],
        out_shape=[jax.ShapeDtypeStruct((t, A_WIDTH), BF16), k_shape, jax.ShapeDtypeStruct((t, A_WIDTH), BF16),
                   v_shape, vx_shape,
                   jax.ShapeDtypeStruct((t, B_WIDTH), F32)],
        compiler_params=_params(1), name="even_in",
    )(x, g, w, *tabs)


def _odd_in_kernel(x_ref, g_ref, w_ref, b_ref, cos_ref, sup_ref, sdn_ref,
                   q_ref, k_ref, v_ref, kk_ref, vv_ref, *, transpose_v):
    tm = x_ref.shape[0]
    hn = _rmsnorm_rows(x_ref[...], g_ref[...]).astype(BF16)
    z = jnp.dot(hn, w_ref[...], preferred_element_type=F32) + b_ref[...]
    cos, s_up, s_dn = cos_ref[...], sup_ref[...], sdn_ref[...]
    nq = C_HEADS * C_DH
    for c in range(nq // LANES):
        sl = slice(c * LANES, (c + 1) * LANES)
        q_ref[:, sl] = (_rope128(z[:, sl], cos, s_up, s_dn) * (C_DH ** -0.5 * LOG2E)).astype(BF16)
    k = _rope128(z[:, nq:nq + LANES], cos, s_up, s_dn)
    v = z[:, nq + LANES:nq + 2 * LANES]
    tail = k_ref.shape[0]
    k_ref[...] = k[tm - tail:]
    v_ref[...] = v[tm - tail:]
    lo = lax.broadcasted_iota(jnp.int32, (tm, LANES), 1) < C_DH
    for src, dst in ((k, kk_ref),) if transpose_v else ((k, kk_ref), (v, vv_ref)):
        h0 = jnp.where(lo, src, 0.0)
        h1 = jnp.where(lo, 0.0, src)
        for n, val in enumerate((h0, pltpu.roll(h0, C_DH, 1), pltpu.roll(h1, C_DH, 1), h1)):
            dst[n] = val.astype(BF16)
    if transpose_v:
        vt = v.T.astype(BF16)
        zero = jnp.zeros((C_DH, tm), BF16)
        vv_ref[0] = jnp.concatenate([vt[:C_DH], zero], axis=0)
        vv_ref[1] = jnp.concatenate([zero, vt[:C_DH]], axis=0)
        vv_ref[2] = jnp.concatenate([vt[C_DH:], zero], axis=0)
        vv_ref[3] = jnp.concatenate([zero, vt[C_DH:]], axis=0)


def _odd_in(x, g, w, b, tabs, tm, nseq, tail, transpose_v):
    t = x.shape[0]
    n_tab = tabs[0].shape[0] // tm
    nt = t // nseq // tm
    nq = C_HEADS * C_DH
    row = lambda i: (i, 0)
    kv_spec = pl.BlockSpec((tail, LANES), lambda i: (i // nt, 0))
    kv_shape = jax.ShapeDtypeStruct((nseq * tail, LANES), F32)
    tab = lambda i: (i % n_tab, 0)
    const = lambda i: (0, 0)
    if transpose_v:
        vv_spec = pl.BlockSpec((4, LANES, tm), lambda i: (0, 0, i))
        vv_shape = jax.ShapeDtypeStruct((4, LANES, t), BF16)
    else:
        vv_spec = pl.BlockSpec((4, tm, LANES), lambda i: (0, i, 0))
        vv_shape = jax.ShapeDtypeStruct((4, t, LANES), BF16)
    return pl.pallas_call(
        functools.partial(_odd_in_kernel, transpose_v=transpose_v),
        grid=(t // tm,),
        in_specs=[pl.BlockSpec((tm, D_MODEL), row), pl.BlockSpec((1, D_MODEL), const),
                  pl.BlockSpec(w.shape, const), pl.BlockSpec((1, w.shape[1]), const),
                  pl.BlockSpec((tm, LANES), tab), pl.BlockSpec((tm, LANES), tab), pl.BlockSpec((tm, LANES), tab)],
        out_specs=[pl.BlockSpec((tm, nq), row), kv_spec, kv_spec,
                   pl.BlockSpec((4, tm, LANES), lambda i: (0, i, 0)), vv_spec],
        out_shape=[jax.ShapeDtypeStruct((t, nq), BF16), kv_shape, kv_shape,
                   jax.ShapeDtypeStruct((4, t, LANES), BF16), vv_shape],
        compiler_params=_params(1), name="odd_in",
    )(x, g, w, b, *tabs)


def _diff_lambda(lam_ref, lam_init):
    lp = lam_ref[...]
    return (jnp.exp(jnp.sum(lp[0:1] * lp[1:2], axis=-1, keepdims=True))
            - jnp.exp(jnp.sum(lp[2:3] * lp[3:4], axis=-1, keepdims=True)) + lam_init)


def _diff_prompt_kernel(q_ref, kb_ref, vt_ref, lam_ref, sub_ref, o_ref, m_ref, l_ref, acc_ref, sa_ref, sb_ref,
                        *, lam_init):
    i = pl.program_id(2)
    tq = q_ref.shape[0]
    tk = tq
    q = q_ref[...].astype(F32)
    lo = lax.broadcasted_iota(jnp.int32, (tq, LANES), 1) < A_DH
    qm = (jnp.where(lo, q, 0.0).astype(BF16), jnp.where(lo, 0.0, q).astype(BF16))
    m_ref[...] = jnp.full(m_ref.shape, -jnp.inf, F32)
    l_ref[...] = jnp.zeros(l_ref.shape, F32)
    acc_ref[...] = jnp.zeros(acc_ref.shape, F32)

    def scores(j, s_ref):
        start = pl.multiple_of(j * tk, tk)
        k = kb_ref[pl.ds(start, tk), :]
        for mp in range(2):
            s_ref[mp] = lax.dot_general(k, qm[mp], NT_DIMS, preferred_element_type=F32)

    def consume(j, s_ref, mask):
        vt = vt_ref[j]
        for mp in range(2):
            s = s_ref[mp]
            if mask is not None:
                s = jnp.where(mask, s, -jnp.inf)
            m_prev = m_ref[mp]
            m_new = jnp.maximum(m_prev, jnp.max(s, axis=0, keepdims=True))
            alpha = jnp.exp2(m_prev - m_new)
            p = jnp.exp2(s - m_new)
            l_ref[mp] = alpha * l_ref[mp] + jnp.sum(p, axis=0, keepdims=True)
            acc_ref[mp] = alpha * acc_ref[mp] + jnp.dot(vt, p.astype(BF16), preferred_element_type=F32)
            m_ref[mp] = m_new

    scores(0, sa_ref)

    def pair(j0):
        scores(j0 + 1, sb_ref)
        consume(j0, sa_ref, None)
        scores(j0 + 2, sa_ref)
        consume(j0 + 1, sb_ref, None)

    def quad(t, carry):
        pair(4 * t)
        pair(4 * t + 2)
        return carry

    n_quads = lax.shift_right_logical(i, 2)
    lax.fori_loop(0, n_quads, quad, 0)

    @pl.when((i & 2) == 2)
    def _():
        pair(4 * n_quads)

    k_chunk = lax.broadcasted_iota(jnp.int32, (tk, tq), 0) // CHUNK
    q_chunk = lax.broadcasted_iota(jnp.int32, (tk, tq), 1) // CHUNK
    diag = k_chunk <= q_chunk
    odd = (i & 1) == 1

    @pl.when(odd)
    def _():
        scores(i, sb_ref)
        consume(i - 1, sa_ref, None)
        consume(i, sb_ref, diag)

    @pl.when(jnp.logical_not(odd))
    def _():
        consume(i, sa_ref, diag)

    lam = _diff_lambda(lam_ref, lam_init)
    a = acc_ref[0] / l_ref[0] - lam * (acc_ref[1] / l_ref[1])
    a = a * lax.rsqrt(jnp.mean(a * a, axis=0, keepdims=True) + EPS) * sub_ref[...] * (1.0 - lam_init)
    o_ref[...] = a.T.astype(BF16)


def _diff_prompt(q, kb, vt, lam_p, sub_col, batch, seq, lam_init):
    t = q.shape[0]
    tq = ATTN_TILE
    nq = seq // tq
    assert vt.shape == (t // tq, A_WIDTH, tq)
    return pl.pallas_call(
        functools.partial(_diff_prompt_kernel, lam_init=lam_init),
        grid=(batch, A_HEADS, nq),
        in_specs=[pl.BlockSpec((tq, LANES), lambda b, h, i: (b * nq + i, h)),
                  pl.BlockSpec((seq, LANES), lambda b, h, i: (b, h)),
                  pl.BlockSpec((nq, LANES, tq), lambda b, h, i: (b, h, 0)),
                  pl.BlockSpec((4, A_DH), lambda b, h, i: (0, 0)),
                  pl.BlockSpec((LANES, 1), lambda b, h, i: (0, 0))],
        out_specs=pl.BlockSpec((tq, LANES), lambda b, h, i: (b * nq + i, h)),
        out_shape=jax.ShapeDtypeStruct((t, A_WIDTH), BF16),
        scratch_shapes=[pltpu.VMEM((2, 1, tq), F32), pltpu.VMEM((2, 1, tq), F32),
                        pltpu.VMEM((2, LANES, tq), F32),
                        pltpu.VMEM((2, tq, tq), F32), pltpu.VMEM((2, tq, tq), F32)],
        compiler_params=_params(3), name="diff_attn_prompt",
    )(q, kb, vt, lam_p, sub_col)


def _diff_sample_kernel(q_ref, kn_ref, vn_ref, ckt_ref, cv_ref, lam_ref, sub_ref, o_ref, *, lam_init):
    sq = q_ref.shape[0]
    past = ckt_ref.shape[3]
    lo = lax.broadcasted_iota(jnp.int32, (sq, LANES), 1) < A_DH
    lam = _diff_lambda(lam_ref, lam_init)
    for h in range(A_HEADS):
        sl = slice(h * LANES, (h + 1) * LANES)
        q = q_ref[:, sl].astype(F32)
        q2 = jnp.concatenate([jnp.where(lo, q, 0.0), jnp.where(lo, 0.0, q)], axis=0).astype(BF16)
        s_c = jnp.dot(q2, ckt_ref[0, 0, sl, :].astype(BF16), preferred_element_type=F32)
        s_n = lax.dot_general(q2, kn_ref[:, sl], NT_DIMS, preferred_element_type=F32)
        m = jnp.maximum(jnp.max(s_c, axis=-1, keepdims=True), jnp.max(s_n, axis=-1, keepdims=True))
        e_c = jnp.exp2(s_c - m)
        e_n = jnp.exp2(s_n - m)
        den = jnp.sum(e_c, axis=-1, keepdims=True) + jnp.sum(e_n, axis=-1, keepdims=True)
        cv = cv_ref[0, 0, pl.ds(h, past, stride=A_HEADS), :].astype(BF16)
        o = (jnp.dot(e_c.astype(BF16), cv, preferred_element_type=F32)
             + jnp.dot(e_n.astype(BF16), vn_ref[:, sl], preferred_element_type=F32)) / den
        a = o[:sq] - lam * o[sq:]
        o_ref[:, sl] = (_rmsnorm_rows(a, sub_ref[...]) * (1.0 - lam_init)).astype(BF16)


def _diff_sample(q, kn, vn, cache_kt, cache_v, layer, lam_p, sub, lam_init):
    _, nb, _, past = cache_kt.shape
    t = q.shape[0]
    sq = t // nb
    row = lambda b: (b, 0)
    return pl.pallas_call(
        functools.partial(_diff_sample_kernel, lam_init=lam_init),
        grid=(nb,),
        in_specs=[pl.BlockSpec((sq, A_WIDTH), row), pl.BlockSpec((sq, A_WIDTH), row), pl.BlockSpec((sq, A_WIDTH), row),
                  pl.BlockSpec((1, 1, A_WIDTH, past), lambda b: (layer, b, 0, 0)),
                  pl.BlockSpec((1, 1, past * A_HEADS, LANES), lambda b: (layer, b, 0, 0)),
                  pl.BlockSpec((4, A_DH), lambda b: (0, 0)),
                  pl.BlockSpec((1, LANES), lambda b: (0, 0))],
        out_specs=pl.BlockSpec((sq, A_WIDTH), row),
        out_shape=jax.ShapeDtypeStruct((t, A_WIDTH), BF16),
        compiler_params=_params(1), name="diff_attn_sample",
    )(q, kn, vn, cache_kt, cache_v, lam_p, sub)


def _pool_kernel(a_ref, u_ref, hist_ref, pw_ref, ps_ref, o_ref, ext_ref, *, n_hist):
    i = pl.program_id(1)
    tm = u_ref.shape[0]

    @pl.when(i == 0)
    def _():
        ext_ref[0:HIST_ROWS, :] = hist_ref[0]

    u = u_ref[...]
    ext_ref[HIST_ROWS:HIST_ROWS + tm, :] = u
    pos = (i * tm + lax.broadcasted_iota(jnp.int32, (tm, 1), 0)).astype(F32)
    o_ref[:, 0:A_WIDTH] = a_ref[...]
    for g, win in enumerate(POOL_WINDOWS):
        sl = slice(g * POOL_CH, (g + 1) * POOL_CH)
        run = ext_ref[:, sl]
        span = 1
        while span < win:
            run = run + pltpu.roll(run, span, 0)
            span *= 2
        tot = run[HIST_ROWS:]
        cnt = jnp.minimum(float(win), pos + (1.0 + n_hist))
        pooled = (tot / cnt - u[:, sl]).astype(BF16)
        y = jnp.dot(pooled, pw_ref[g], preferred_element_type=F32) * ps_ref[:, sl]
        o_ref[:, A_WIDTH + g * POOL_CH:A_WIDTH + (g + 1) * POOL_CH] = y.astype(BF16)
    ext_ref[0:HIST_ROWS, :] = ext_ref[tm:tm + HIST_ROWS, :]


def _pool(a, u, hist, pool_w, pool_scale, nseq, tm, n_hist):
    t = a.shape[0]
    nt = t // nseq // tm
    row = lambda b, i: (b * nt + i, 0)
    return pl.pallas_call(
        functools.partial(_pool_kernel, n_hist=n_hist),
        grid=(nseq, nt),
        in_specs=[pl.BlockSpec((tm, A_WIDTH), row), pl.BlockSpec((tm, B_WIDTH), row),
                  pl.BlockSpec((1, HIST_ROWS, B_WIDTH), lambda b, i: (b, 0, 0)),
                  pl.BlockSpec(pool_w.shape, lambda b, i: (0, 0, 0)),
                  pl.BlockSpec((1, B_WIDTH), lambda b, i: (0, 0))],
        out_specs=pl.BlockSpec((tm, D_MODEL), row),
        out_shape=jax.ShapeDtypeStruct((t, D_MODEL), BF16),
        scratch_shapes=[pltpu.VMEM((HIST_ROWS + tm, B_WIDTH), F32)],
        compiler_params=_params(2), name="pool_mix",
    )(a, u, hist, pool_w, pool_scale)


def _swa_bias(tq):
    nk, nq = 2 * tq, 4 * tq
    kc = (jnp.arange(2 * nk) % nk // CHUNK)[:, None]
    qc = (jnp.arange(nq) % tq // CHUNK)[None, :]
    band = (kc >= qc) & (kc <= qc + 2)
    return jnp.where(jnp.stack([band & (kc >= 2), band]), 0.0, -jnp.inf).astype(F32)


def _swa_prompt_kernel(sink_ref, q_ref, kp_ref, kc_ref, vp_ref, vc_ref, bias_ref, o_ref, sa_ref, sb_ref):
    tq = q_ref.shape[0]
    nk = 2 * tq
    nq = 4 * tq
    pair = lax.broadcasted_iota(jnp.int32, (1, nq), 1) // tq
    low_rows = lax.broadcasted_iota(jnp.int32, (LANES, 1), 0) < C_DH

    def scores(kv, s_ref):
        qs = jnp.concatenate([q_ref[:, (kv * 4 + p) * LANES:(kv * 4 + p + 1) * LANES] for p in range(4)], axis=0)
        ks = jnp.concatenate([kp_ref[2 * kv], kc_ref[2 * kv], kp_ref[2 * kv + 1], kc_ref[2 * kv + 1]], axis=0)
        s_ref[...] = lax.dot_general(ks, qs, NT_DIMS, preferred_element_type=F32)

    def attend(kv, s_ref):
        vt = jnp.concatenate([vp_ref[2 * kv], vc_ref[2 * kv], vp_ref[2 * kv + 1], vc_ref[2 * kv + 1]], axis=1)
        s = s_ref[...] + bias_ref[0]
        es, rdens = [], []
        for half in range(2):
            sink = jnp.zeros((1, nq), F32)
            for p in range(4):
                sink = jnp.where(pair == p, sink_ref[kv * 8 + 2 * p + half] * LOG2E, sink)
            sh = s[half * nk:(half + 1) * nk]
            m = jnp.maximum(jnp.max(sh, axis=0, keepdims=True), sink)
            e = jnp.exp2(sh - m)
            rdens.append(1.0 / (jnp.sum(e, axis=0, keepdims=True) + jnp.exp2(sink - m)))
            es.append(e.astype(BF16))
        o = jnp.dot(vt, jnp.concatenate(es, axis=0), preferred_element_type=F32)
        o = (o * jnp.where(low_rows, rdens[0], rdens[1])).T
        for pr in range(4):
            o_ref[:, (kv * 4 + pr) * LANES:(kv * 4 + pr + 1) * LANES] = o[pr * tq:(pr + 1) * tq].astype(BF16)

    scores(0, sa_ref)
    scores(1, sb_ref)
    attend(0, sa_ref)
    attend(1, sb_ref)


def _swa_prompt(sinks, q, kk, vvt, batch, seq):
    t = q.shape[0]
    tq = SWA_TILE
    nt = seq // tq
    kspec = lambda f: pl.BlockSpec((4, tq, LANES), f)
    vspec = lambda f: pl.BlockSpec((4, LANES, tq), f)
    return pl.pallas_call(
        _swa_prompt_kernel,
        grid=(batch, nt),
        in_specs=[pl.BlockSpec(memory_space=pltpu.SMEM),
                  pl.BlockSpec((tq, C_HEADS * C_DH), lambda b, i: (b * nt + i, 0)),
                  kspec(lambda b, i: (0, b * nt + jnp.maximum(i - 1, 0), 0)), kspec(lambda b, i: (0, b * nt + i, 0)),
                  vspec(lambda b, i: (0, 0, b * nt + jnp.maximum(i - 1, 0))), vspec(lambda b, i: (0, 0, b * nt + i)),
                  pl.BlockSpec((1, 4 * tq, 4 * tq), lambda b, i: (jnp.minimum(i, 1), 0, 0))],
        out_specs=pl.BlockSpec((tq, C_HEADS * C_DH), lambda b, i: (b * nt + i, 0)),
        out_shape=jax.ShapeDtypeStruct((t, C_HEADS * C_DH), BF16),
        scratch_shapes=[pltpu.VMEM((4 * tq, 4 * tq), F32), pltpu.VMEM((4 * tq, 4 * tq), F32)],
        compiler_params=_params(2), name="swa_prompt",
    )(sinks, q, kk, kk, vvt, vvt, _swa_bias(tq))


def _swa_sample_kernel(sink_ref, q_ref, kk_ref, vv_ref, ck_ref, cv_ref, o_ref):
    sq = q_ref.shape[0]
    nc = ck_ref.shape[1]
    lo = lax.broadcasted_iota(jnp.int32, (nc, LANES), 1) < C_DH

    def halves(c, kv):
        own = jnp.where(lo, c, 0.0) if kv == 0 else jnp.where(lo, 0.0, c)
        swapped = pltpu.roll(own, C_DH, 1)
        return ((own, swapped) if kv == 0 else (swapped, own))

    ck = ck_ref[0]
    cv = cv_ref[0]
    for kv in range(C_KV):
        qs = jnp.concatenate([q_ref[:, (kv * 4 + p) * LANES:(kv * 4 + p + 1) * LANES] for p in range(4)], axis=0)
        ckh = halves(ck, kv)
        cvh = halves(cv, kv)
        o = jnp.zeros((4 * sq, LANES), F32)
        for half in range(2):
            ks = jnp.concatenate([ckh[half].astype(BF16), kk_ref[2 * kv + half]], axis=0)
            vs = jnp.concatenate([cvh[half].astype(BF16), vv_ref[2 * kv + half]], axis=0)
            s = lax.dot_general(qs, ks, NT_DIMS, preferred_element_type=F32)
            sink = jnp.concatenate(
                [jnp.full((sq, 1), sink_ref[kv * 8 + 2 * p + half] * LOG2E, F32) for p in range(4)], axis=0)
            m = jnp.maximum(jnp.max(s, axis=-1, keepdims=True), sink)
            e = jnp.exp2(s - m)
            p = e / (jnp.sum(e, axis=-1, keepdims=True) + jnp.exp2(sink - m))
            o = o + jnp.dot(p.astype(BF16), vs, preferred_element_type=F32)
        for pr in range(4):
            o_ref[:, (kv * 4 + pr) * LANES:(kv * 4 + pr + 1) * LANES] = o[pr * sq:(pr + 1) * sq].astype(BF16)


def _swa_sample(sinks, q, kk, vv, cache_k, cache_v):
    nb, nc, _ = cache_k.shape
    t = q.shape[0]
    sq = t // nb
    return pl.pallas_call(
        _swa_sample_kernel,
        grid=(nb,),
        in_specs=[pl.BlockSpec(memory_space=pltpu.SMEM),
                  pl.BlockSpec((sq, C_HEADS * C_DH), lambda b: (b, 0)),
                  pl.BlockSpec((4, sq, LANES), lambda b: (0, b, 0)),
                  pl.BlockSpec((4, sq, LANES), lambda b: (0, b, 0)),
                  pl.BlockSpec((1, nc, LANES), lambda b: (b, 0, 0)),
                  pl.BlockSpec((1, nc, LANES), lambda b: (b, 0, 0))],
        out_specs=pl.BlockSpec((sq, C_HEADS * C_DH), lambda b: (b, 0)),
        out_shape=jax.ShapeDtypeStruct((t, C_HEADS * C_DH), BF16),
        compiler_params=_params(1), name="swa_sample",
    )(sinks, q, kk, vv, cache_k, cache_v)


def _shift_rows(up, prev, shift, nseq):
    seq_len = up.shape[0] // nseq
    rolled = pltpu.roll(up, shift, 0)
    first = lax.broadcasted_iota(jnp.int32, (SUBLANES, up.shape[1]), 0) < shift
    pieces = []
    for s in range(nseq):
        lo = s * seq_len
        head = jnp.where(first, pltpu.roll(prev[s], shift, 0), pltpu.roll(up[lo:lo + SUBLANES], shift, 0))
        pieces += [head, rolled[lo + SUBLANES:lo + seq_len]]
    return jnp.concatenate(pieces, axis=0)


def _ffn_kernel(x_ref, mix_ref, wo_ref, g_ref, hist_ref, wu_ref, cw_ref, cb_ref, wd_ref, gf_ref,
                y_ref, st_ref, hn_ref, h_ref, carry_ref, *, nseq, final_norm):
    i = pl.program_id(1)
    tm = x_ref.shape[0]
    seq_len = tm // nseq

    @pl.when(i == 0)
    def _():
        carry_ref[...] = hist_ref[0]

    x = x_ref[...] + jnp.dot(mix_ref[...], wo_ref[0], preferred_element_type=F32)
    hn_ref[...] = _rmsnorm_rows(x, g_ref[...]).astype(BF16)
    for c in range(N_FF_CHUNKS):
        conv = []
        for sl in (slice(c * FF_CHUNK, (c + 1) * FF_CHUNK), slice(D_FF + c * FF_CHUNK, D_FF + (c + 1) * FF_CHUNK)):
            up = jnp.dot(hn_ref[...], wu_ref[0, :, sl], preferred_element_type=F32)
            prev = [carry_ref[s, :, sl] for s in range(nseq)]
            conv.append(cb_ref[0, :, sl] + _shift_rows(up, prev, 2, nseq) * cw_ref[0, 0:1, sl]
                        + _shift_rows(up, prev, 1, nseq) * cw_ref[0, 1:2, sl] + up * cw_ref[0, 2:3, sl])
            for s in range(nseq):
                carry_ref[s, :, sl] = up[(s + 1) * seq_len - SUBLANES:(s + 1) * seq_len]
        gate, val = conv
        h_ref[:, c * FF_CHUNK:(c + 1) * FF_CHUNK] = (gate * jax.nn.sigmoid(gate) * val).astype(BF16)
    y = x + jnp.dot(h_ref[...], wd_ref[0], preferred_element_type=F32)
    if final_norm:
        y = _rmsnorm_rows(y, gf_ref[...])
    y_ref[...] = y
    st_ref[0] = carry_ref[...]


def _ffn(x, mix, wo, mix_layer, g, hist, wu, cw, cb, wd, layer, g_final, nseq, tm, final_norm):
    t = x.shape[0]
    n_outer = hist.shape[0]
    nt = t // n_outer // tm
    row = lambda b, i: (b * nt + i, 0)
    c2 = lambda b, i: (0, 0)
    lay = lambda b, i: (layer, 0, 0)
    once = pl.Buffered(1)
    st_block = (1,) + hist.shape[1:]
    st_map = lambda b, i: (b, 0, 0, 0)
    return pl.pallas_call(
        functools.partial(_ffn_kernel, nseq=nseq, final_norm=final_norm),
        grid=(n_outer, nt),
        in_specs=[pl.BlockSpec((tm, D_MODEL), row), pl.BlockSpec((tm, D_MODEL), row),
                  pl.BlockSpec((1,) + wo.shape[1:], lambda b, i: (mix_layer, 0, 0), pipeline_mode=once),
                  pl.BlockSpec((1, D_MODEL), c2),
                  pl.BlockSpec(st_block, st_map),
                  pl.BlockSpec((1,) + wu.shape[1:], lay, pipeline_mode=once), pl.BlockSpec((1,) + cw.shape[1:], lay),
                  pl.BlockSpec((1,) + cb.shape[1:], lay), pl.BlockSpec((1,) + wd.shape[1:], lay, pipeline_mode=once),
                  pl.BlockSpec((1, D_MODEL), c2)],
        out_specs=[pl.BlockSpec((tm, D_MODEL), row), pl.BlockSpec(st_block, st_map)],
        out_shape=[jax.ShapeDtypeStruct((t, D_MODEL), F32), jax.ShapeDtypeStruct(hist.shape, F32)],
        scratch_shapes=[pltpu.VMEM((tm, D_MODEL), BF16),
                        pltpu.VMEM((tm, D_FF), BF16),
                        pltpu.VMEM(hist.shape[1:], F32)],
        compiler_params=_params(2), name="conv_ffn",
    )(x, mix, wo, g, hist, wu, cw, cb, wd, g_final)


def kernel(x_prompt, x_sample, cache_diff_k, cache_diff_v, state_pool, cache_swa_k, cache_swa_v, state_ffn_conv,
           norm_attn, norm_ffn, norm_final, w_in_even, w_out_even, diff_lambda, diff_subln, pool_w, pool_scale,
           w_in_odd, b_in_odd, w_out_odd, sinks, w_up, conv_w, conv_b, w_down):
    bp, sp, d = x_prompt.shape
    bs, ss, _ = x_sample.shape
    depth = norm_attn.shape[0]
    past = cache_diff_k.shape[2]
    hp = x_prompt.reshape(bp * sp, d)
    hs = x_sample.reshape(bs * ss, d)
    tm_s = bs * ss

    tabs_p = _rope_tables(jnp.arange(sp, dtype=jnp.int32))
    tabs_s = tuple(jnp.tile(t, (bs, 1)) for t in _rope_tables(past + jnp.arange(ss, dtype=jnp.int32)))

    wu_b = w_up.astype(BF16)
    wo_even = w_out_even.astype(BF16)
    wo_odd = w_out_odd.astype(BF16)
    wd_b = w_down.astype(BF16)
    cb3 = conv_b[:, None, :]
    conv_pad = jnp.zeros((bs, SUBLANES - (CONV_W - 1), 2 * D_FF), F32)
    zero_conv = jnp.zeros((bp, 1, SUBLANES, 2 * D_FF), F32)
    zero_pool = jnp.zeros((bp, HIST_ROWS, B_WIDTH), F32)
    n_even = cache_diff_k.shape[0]
    cache_kt = jnp.transpose(cache_diff_k, (0, 1, 3, 4, 5, 2)).reshape(n_even, bs, A_WIDTH, past)
    cache_vr = cache_diff_v.reshape(n_even, bs, past * A_HEADS, 2 * A_DH)

    dkp, dvp, plp, skp, svp, fcp = [], [], [], [], [], []
    dks, dvs, pls, sks, svs, fcs = [], [], [], [], [], []
    for i in range(depth):
        j = i // 2
        g_attn = norm_attn[i][None, :]
        if i % 2 == 0:
            lam_init = 0.8 - 0.6 * math.exp(-0.3 * i)
            w_in = w_in_even[j].astype(BF16)
            pw = pool_w[j].astype(BF16)
            ps = pool_scale[j][None, :]
            lam_p = diff_lambda[j]
            wo = wo_even

            q, kt, kb, v, vt, u = _even_in(hp, g_attn, w_in, tabs_p, ROW_TILE, bp, True)
            a = _diff_prompt(q, kb, vt, lam_p, diff_subln[j][:, None], bp, sp, lam_init)
            mix_p = _pool(a, u, zero_pool, pw, ps, bp, ROW_TILE, 0)
            dkp.append(kt)
            dvp.append(v.reshape(bp, sp, A_HEADS, 2 * A_DH))
            plp.append(u.reshape(bp, sp, B_WIDTH)[:, sp - POOL_HIST:])

            q, k, kb, v, vb, u = _even_in(hs, g_attn, w_in, tabs_s, tm_s, 1, False)
            a = _diff_sample(q, kb, vb, cache_kt, cache_vr, j, lam_p, diff_subln[j][None, :], lam_init)
            hist = jnp.concatenate([jnp.zeros((bs, HIST_ROWS - POOL_HIST, B_WIDTH), F32), state_pool[j]], axis=1)
            mix_s = _pool(a, u, hist, pw, ps, bs, ss, POOL_HIST)
            dks.append(k.reshape(bs, ss, A_HEADS, 2, A_DH))
            dvs.append(v.reshape(bs, ss, A_HEADS, 2 * A_DH))
            pls.append(u.reshape(bs, ss, B_WIDTH)[:, ss - POOL_HIST:])
        else:
            w_in = w_in_odd[j].astype(BF16)
            b_in = b_in_odd[j][None, :]
            sk = sinks[j]
            wo = wo_odd

            q, k, v, kk, vvt = _odd_in(hp, g_attn, w_in, b_in, tabs_p, ROW_TILE, bp, C_CACHE, True)
            mix_p = _swa_prompt(sk, q, kk, vvt, bp, sp)
            skp.append(k.reshape(bp, C_CACHE, C_KV, C_DH))
            svp.append(v.reshape(bp, C_CACHE, C_KV, C_DH))

            q, k, v, kk, vv = _odd_in(hs, g_attn, w_in, b_in, tabs_s, tm_s, 1, tm_s, False)
            mix_s = _swa_sample(sk, q, kk, vv, cache_swa_k[j].reshape(bs, C_CACHE, C_KV * C_DH),
                                cache_swa_v[j].reshape(bs, C_CACHE, C_KV * C_DH))
            k_all = jnp.concatenate([cache_swa_k[j], k.reshape(bs, ss, C_KV, C_DH)], axis=1)
            v_all = jnp.concatenate([cache_swa_v[j], v.reshape(bs, ss, C_KV, C_DH)], axis=1)
            sks.append(k_all[:, -C_CACHE:])
            svs.append(v_all[:, -C_CACHE:])

        last = i == depth - 1
        g_ffn = norm_ffn[i][None, :]
        g_fin = norm_final[None, :]
        hp, st = _ffn(hp, mix_p, wo, j, g_ffn, zero_conv, wu_b, conv_w, cb3, wd_b, i, g_fin, 1, ROW_TILE, last)
        fcp.append(st[:, 0, SUBLANES - (CONV_W - 1):])
        hist = jnp.concatenate([conv_pad, state_ffn_conv[i]], axis=1)[None]
        hs, st = _ffn(hs, mix_s, wo, j, g_ffn, hist, wu_b, conv_w, cb3, wd_b, i, g_fin, bs, tm_s, last)
        fcs.append(st[0, :, SUBLANES - (CONV_W - 1):])

    diff_k_prompt = jnp.transpose(jnp.stack(dkp).reshape(n_even, bp, A_HEADS, 2, A_DH, sp), (0, 1, 5, 2, 3, 4))
    return (hp.reshape(bp, sp, d), hs.reshape(bs, ss, d),
            diff_k_prompt, jnp.stack(dvp), jnp.stack(plp), jnp.stack(skp), jnp.stack(svp), jnp.stack(fcp),
            jnp.stack(dks), jnp.stack(dvs), jnp.stack(pls), jnp.stack(sks), jnp.stack(svs), jnp.stack(fcs))
```

```python
import functools
import math

import jax
import jax.numpy as jnp
from jax import lax
from jax.experimental import pallas as pl
from jax.experimental.pallas import tpu as pltpu

F32 = jnp.float32
BF16 = jnp.bfloat16

D_MODEL = 1024
CHUNK = 64
ROPE_THETA = 10000.0
EPS = 1e-5
A_HEADS = 4
A_DH = 64
A_WIDTH = A_HEADS * 2 * A_DH
POOL_WINDOWS = (2, 4, 8, 16)
POOL_CH = 128
POOL_HIST = 15
B_WIDTH = 512
C_HEADS = 16
C_KV = 2
C_DH = 64
C_CACHE = 128
D_FF = 2816
CONV_W = 3

LOG2E = math.log2(math.e)
LANES = 128
SUBLANES = 8
HIST_ROWS = 16
FF_CHUNK = 256
N_FF_CHUNKS = D_FF // FF_CHUNK
ROW_TILE = 512
ATTN_TILE = 512
SWA_TILE = 2 * CHUNK
VMEM_LIMIT = 56 * 1024 * 1024

NT_DIMS = (((1,), (1,)), ((), ()))


def _params(n_axes, vmem=VMEM_LIMIT):
    return pltpu.CompilerParams(dimension_semantics=("arbitrary",) * n_axes, vmem_limit_bytes=vmem)


def _rmsnorm_rows(x, g):
    return x * lax.rsqrt(jnp.mean(x * x, axis=-1, keepdims=True) + EPS) * g


def _rope128(z, cos, s_up, s_dn):
    return z * cos + pltpu.roll(z, 96, 1) * s_up + pltpu.roll(z, 32, 1) * s_dn


def _rope_tables(pos):
    inv = ROPE_THETA ** (-jnp.arange(0, A_DH, 2, dtype=F32) / A_DH)
    ang = pos.astype(F32)[:, None] * inv[None, :]
    cos, sin = jnp.cos(ang), jnp.sin(ang)
    zero = jnp.zeros_like(sin)
    cos128 = jnp.tile(cos, (1, 4))
    s_up = jnp.tile(jnp.concatenate([-sin, zero], axis=1), (1, 2))
    s_dn = jnp.tile(jnp.concatenate([zero, sin], axis=1), (1, 2))
    return cos128, s_up, s_dn


def _even_in_kernel(x_ref, g_ref, w_ref, cos_ref, sup_ref, sdn_ref,
                    q_ref, k_ref, kb_ref, v_ref, vx_ref, u_ref, *, prompt):
    tm = x_ref.shape[0]
    hn = _rmsnorm_rows(x_ref[...], g_ref[...]).astype(BF16)
    z = jnp.dot(hn, w_ref[...], preferred_element_type=F32)
    cos, s_up, s_dn = cos_ref[...], sup_ref[...], sdn_ref[...]
    for c in range(A_HEADS):
        sl = slice(c * LANES, (c + 1) * LANES)
        rq = _rope128(z[:, sl], cos, s_up, s_dn)
        q_ref[:, sl] = (rq * (A_DH ** -0.5 * LOG2E)).astype(BF16)
        rk = _rope128(z[:, A_WIDTH + c * LANES:A_WIDTH + (c + 1) * LANES], cos, s_up, s_dn)
        kb_ref[:, sl] = rk.astype(BF16)
        if prompt:
            k_ref[0, sl, :] = rk.T
        else:
            k_ref[:, sl] = rk
    v = z[:, 2 * A_WIDTH:3 * A_WIDTH]
    if prompt:
        for c in range(A_HEADS):
            v_ref[pl.ds(c, tm, stride=A_HEADS), :] = v[:, c * LANES:(c + 1) * LANES]
        vx_ref[0] = v.T.astype(BF16)
    else:
        v_ref[...] = v
        vx_ref[...] = v.astype(BF16)
    u_ref[...] = z[:, 3 * A_WIDTH:]


def _even_in(x, g, w, tabs, tm, nseq, prompt):
    t = x.shape[0]
    n_tab = tabs[0].shape[0] // tm
    nt = t // nseq // tm
    row = lambda i: (i, 0)
    tab = lambda i: (i % n_tab, 0)
    const = lambda i: (0, 0)
    if prompt:
        k_spec = pl.BlockSpec((1, A_WIDTH, tm), lambda i: (i // nt, 0, i % nt))
        k_shape = jax.ShapeDtypeStruct((nseq, A_WIDTH, t // nseq), F32)
        v_spec = pl.BlockSpec((tm * A_HEADS, LANES), row)
        v_shape = jax.ShapeDtypeStruct((t * A_HEADS, LANES), F32)
        vx_spec = pl.BlockSpec((1, A_WIDTH, tm), lambda i: (i, 0, 0))
        vx_shape = jax.ShapeDtypeStruct((t // tm, A_WIDTH, tm), BF16)
    else:
        k_spec = pl.BlockSpec((tm, A_WIDTH), row)
        k_shape = jax.ShapeDtypeStruct((t, A_WIDTH), F32)
        v_spec = pl.BlockSpec((tm, A_WIDTH), row)
        v_shape = jax.ShapeDtypeStruct((t, A_WIDTH), F32)
        vx_spec = pl.BlockSpec((tm, A_WIDTH), row)
        vx_shape = jax.ShapeDtypeStruct((t, A_WIDTH), BF16)
    return pl.pallas_call(
        functools.partial(_even_in_kernel, prompt=prompt),
        grid=(t // tm,),
        in_specs=[pl.BlockSpec((tm, D_MODEL), row), pl.BlockSpec((1, D_MODEL), const),
                  pl.BlockSpec(w.shape, const),
                  pl.BlockSpec((tm, LANES), tab), pl.BlockSpec((tm, LANES), tab), pl.BlockSpec((tm, LANES), tab)],
        out_specs=[pl.BlockSpec((tm, A_WIDTH), row), k_spec, pl.BlockSpec((tm, A_WIDTH), row),
                   v_spec, vx_spec,
                   pl.BlockSpec((tm, B_WIDTH), row)],
        out_shape=[jax.ShapeDtypeStruct((t, A_WIDTH), BF16), k_shape, jax.ShapeDtypeStruct((t, A_WIDTH), BF16),
                   v_shape, vx_shape,
                   jax.ShapeDtypeStruct((t, B_WIDTH), F32)],
        compiler_params=_params(1), name="even_in",
    )(x, g, w, *tabs)


def _odd_in_kernel(x_ref, g_ref, w_ref, b_ref, cos_ref, sup_ref, sdn_ref,
                   q_ref, k_ref, v_ref, kk_ref, vv_ref, *, transpose_v):
    tm = x_ref.shape[0]
    hn = _rmsnorm_rows(x_ref[...], g_ref[...]).astype(BF16)
    z = jnp.dot(hn, w_ref[...], preferred_element_type=F32) + b_ref[...]
    cos, s_up, s_dn = cos_ref[...], sup_ref[...], sdn_ref[...]
    nq = C_HEADS * C_DH
    for c in range(nq // LANES):
        sl = slice(c * LANES, (c + 1) * LANES)
        q_ref[:, sl] = (_rope128(z[:, sl], cos, s_up, s_dn) * (C_DH ** -0.5 * LOG2E)).astype(BF16)
    k = _rope128(z[:, nq:nq + LANES], cos, s_up, s_dn)
    v = z[:, nq + LANES:nq + 2 * LANES]
    tail = k_ref.shape[0]
    k_ref[...] = k[tm - tail:]
    v_ref[...] = v[tm - tail:]
    lo = lax.broadcasted_iota(jnp.int32, (tm, LANES), 1) < C_DH
    for src, dst in ((k, kk_ref),) if transpose_v else ((k, kk_ref), (v, vv_ref)):
        h0 = jnp.where(lo, src, 0.0)
        h1 = jnp.where(lo, 0.0, src)
        for n, val in enumerate((h0, pltpu.roll(h0, C_DH, 1), pltpu.roll(h1, C_DH, 1), h1)):
            dst[n] = val.astype(BF16)
    if transpose_v:
        vt = v.T.astype(BF16)
        zero = jnp.zeros((C_DH, tm), BF16)
        vv_ref[0] = jnp.concatenate([vt[:C_DH], zero], axis=0)
        vv_ref[1] = jnp.concatenate([zero, vt[:C_DH]], axis=0)
        vv_ref[2] = jnp.concatenate([vt[C_DH:], zero], axis=0)
        vv_ref[3] = jnp.concatenate([zero, vt[C_DH:]], axis=0)


def _odd_in(x, g, w, b, tabs, tm, nseq, tail, transpose_v):
    t = x.shape[0]
    n_tab = tabs[0].shape[0] // tm
    nt = t // nseq // tm
    nq = C_HEADS * C_DH
    row = lambda i: (i, 0)
    kv_spec = pl.BlockSpec((tail, LANES), lambda i: (i // nt, 0))
    kv_shape = jax.ShapeDtypeStruct((nseq * tail, LANES), F32)
    tab = lambda i: (i % n_tab, 0)
    const = lambda i: (0, 0)
    if transpose_v:
        vv_spec = pl.BlockSpec((4, LANES, tm), lambda i: (0, 0, i))
        vv_shape = jax.ShapeDtypeStruct((4, LANES, t), BF16)
    else:
        vv_spec = pl.BlockSpec((4, tm, LANES), lambda i: (0, i, 0))
        vv_shape = jax.ShapeDtypeStruct((4, t, LANES), BF16)
    return pl.pallas_call(
        functools.partial(_odd_in_kernel, transpose_v=transpose_v),
        grid=(t // tm,),
        in_specs=[pl.BlockSpec((tm, D_MODEL), row), pl.BlockSpec((1, D_MODEL), const),
                  pl.BlockSpec(w.shape, const), pl.BlockSpec((1, w.shape[1]), const),
                  pl.BlockSpec((tm, LANES), tab), pl.BlockSpec((tm, LANES), tab), pl.BlockSpec((tm, LANES), tab)],
        out_specs=[pl.BlockSpec((tm, nq), row), kv_spec, kv_spec,
                   pl.BlockSpec((4, tm, LANES), lambda i: (0, i, 0)), vv_spec],
        out_shape=[jax.ShapeDtypeStruct((t, nq), BF16), kv_shape, kv_shape,
                   jax.ShapeDtypeStruct((4, t, LANES), BF16), vv_shape],
        compiler_params=_params(1), name="odd_in",
    )(x, g, w, b, *tabs)


def _diff_lambda(lam_ref, lam_init):
    lp = lam_ref[...]
    return (jnp.exp(jnp.sum(lp[0:1] * lp[1:2], axis=-1, keepdims=True))
            - jnp.exp(jnp.sum(lp[2:3] * lp[3:4], axis=-1, keepdims=True)) + lam_init)


def _diff_prompt_kernel(q_ref, kb_ref, vt_ref, lam_ref, sub_ref, o_ref, m_ref, l_ref, acc_ref, sa_ref, sb_ref,
                        *, lam_init):
    i = pl.program_id(2)
    tq = q_ref.shape[0]
    tk = vt_ref.shape[2]
    q = q_ref[...].astype(F32)
    lo = lax.broadcasted_iota(jnp.int32, (tq, LANES), 1) < A_DH
    qm = (jnp.where(lo, q, 0.0).astype(BF16), jnp.where(lo, 0.0, q).astype(BF16))
    m_ref[...] = jnp.full(m_ref.shape, -jnp.inf, F32)
    l_ref[...] = jnp.zeros(l_ref.shape, F32)
    acc_ref[...] = jnp.zeros(acc_ref.shape, F32)

    def scores(j, s_ref, cols):
        start = pl.multiple_of(j * tk, tk)
        k = kb_ref[pl.ds(start, tk), :]
        for mp in range(2):
            s_ref[mp, :, cols] = lax.dot_general(k, qm[mp][cols], NT_DIMS, preferred_element_type=F32)

    def consume(j, s_ref, mask, cols):
        vt = vt_ref[j]
        for mp in range(2):
            s = s_ref[mp, :, cols]
            if mask is not None:
                s = jnp.where(mask, s, -jnp.inf)
            m_prev = m_ref[mp, :, cols]
            m_new = jnp.maximum(m_prev, jnp.max(s, axis=0, keepdims=True))
            alpha = jnp.exp2(m_prev - m_new)
            p = jnp.exp2(s - m_new)
            l_ref[mp, :, cols] = alpha * l_ref[mp, :, cols] + jnp.sum(p, axis=0, keepdims=True)
            acc_ref[mp, :, cols] = (alpha * acc_ref[mp, :, cols]
                                    + jnp.dot(vt, p.astype(BF16), preferred_element_type=F32))
            m_ref[mp, :, cols] = m_new

    every = slice(0, tq)
    second = slice(tk, tq)
    scores(0, sa_ref, every)

    def pair(j0):
        scores(j0 + 1, sb_ref, every)
        consume(j0, sa_ref, None, every)
        scores(j0 + 2, sa_ref, every)
        consume(j0 + 1, sb_ref, None, every)

    def quad(t, carry):
        pair(4 * t)
        pair(4 * t + 2)
        return carry

    n_quads = lax.shift_right_logical(i, 1)
    lax.fori_loop(0, n_quads, quad, 0)

    @pl.when((i & 1) == 1)
    def _():
        pair(4 * n_quads)

    k_chunk = lax.broadcasted_iota(jnp.int32, (tk, tq), 0) // CHUNK
    q_chunk = lax.broadcasted_iota(jnp.int32, (tk, tq), 1) // CHUNK
    causal = k_chunk <= q_chunk
    scores(2 * i + 1, sb_ref, second)
    consume(2 * i, sa_ref, causal, every)
    consume(2 * i + 1, sb_ref, causal[:, :tk], second)

    lam = _diff_lambda(lam_ref, lam_init)
    a = acc_ref[0] / l_ref[0] - lam * (acc_ref[1] / l_ref[1])
    a = a * lax.rsqrt(jnp.mean(a * a, axis=0, keepdims=True) + EPS) * sub_ref[...] * (1.0 - lam_init)
    o_ref[...] = a.T.astype(BF16)


def _diff_prompt(q, kb, vt, lam_p, sub_col, batch, seq, lam_init):
    t = q.shape[0]
    tk = ATTN_TILE
    tq = 2 * tk
    nq = seq // tq
    assert vt.shape == (t // tk, A_WIDTH, tk)
    return pl.pallas_call(
        functools.partial(_diff_prompt_kernel, lam_init=lam_init),
        grid=(batch, A_HEADS, nq),
        in_specs=[pl.BlockSpec((tq, LANES), lambda b, h, i: (b * nq + i, h)),
                  pl.BlockSpec((seq, LANES), lambda b, h, i: (b, h)),
                  pl.BlockSpec((seq // tk, LANES, tk), lambda b, h, i: (b, h, 0)),
                  pl.BlockSpec((4, A_DH), lambda b, h, i: (0, 0)),
                  pl.BlockSpec((LANES, 1), lambda b, h, i: (0, 0))],
        out_specs=pl.BlockSpec((tq, LANES), lambda b, h, i: (b * nq + i, h)),
        out_shape=jax.ShapeDtypeStruct((t, A_WIDTH), BF16),
        scratch_shapes=[pltpu.VMEM((2, 1, tq), F32), pltpu.VMEM((2, 1, tq), F32),
                        pltpu.VMEM((2, LANES, tq), F32),
                        pltpu.VMEM((2, tk, tq), F32), pltpu.VMEM((2, tk, tq), F32)],
        compiler_params=_params(3), name="diff_attn_prompt",
    )(q, kb, vt, lam_p, sub_col)


def _diff_sample_kernel(q_ref, kn_ref, vn_ref, ckt_ref, cv_ref, lam_ref, sub_ref, o_ref, *, lam_init):
    sq = q_ref.shape[0]
    past = ckt_ref.shape[3]
    lo = lax.broadcasted_iota(jnp.int32, (sq, LANES), 1) < A_DH
    lam = _diff_lambda(lam_ref, lam_init)
    for h in range(A_HEADS):
        sl = slice(h * LANES, (h + 1) * LANES)
        q = q_ref[:, sl].astype(F32)
        q2 = jnp.concatenate([jnp.where(lo, q, 0.0), jnp.where(lo, 0.0, q)], axis=0).astype(BF16)
        s_c = jnp.dot(q2, ckt_ref[0, 0, sl, :].astype(BF16), preferred_element_type=F32)
        s_n = lax.dot_general(q2, kn_ref[:, sl], NT_DIMS, preferred_element_type=F32)
        m = jnp.maximum(jnp.max(s_c, axis=-1, keepdims=True), jnp.max(s_n, axis=-1, keepdims=True))
        e_c = jnp.exp2(s_c - m)
        e_n = jnp.exp2(s_n - m)
        den = jnp.sum(e_c, axis=-1, keepdims=True) + jnp.sum(e_n, axis=-1, keepdims=True)
        cv = cv_ref[0, 0, pl.ds(h, past, stride=A_HEADS), :].astype(BF16)
        o = (jnp.dot(e_c.astype(BF16), cv, preferred_element_type=F32)
             + jnp.dot(e_n.astype(BF16), vn_ref[:, sl], preferred_element_type=F32)) / den
        a = o[:sq] - lam * o[sq:]
        o_ref[:, sl] = (_rmsnorm_rows(a, sub_ref[...]) * (1.0 - lam_init)).astype(BF16)


def _diff_sample(q, kn, vn, cache_kt, cache_v, layer, lam_p, sub, lam_init):
    _, nb, _, past = cache_kt.shape
    t = q.shape[0]
    sq = t // nb
    row = lambda b: (b, 0)
    return pl.pallas_call(
        functools.partial(_diff_sample_kernel, lam_init=lam_init),
        grid=(nb,),
        in_specs=[pl.BlockSpec((sq, A_WIDTH), row), pl.BlockSpec((sq, A_WIDTH), row), pl.BlockSpec((sq, A_WIDTH), row),
                  pl.BlockSpec((1, 1, A_WIDTH, past), lambda b: (layer, b, 0, 0)),
                  pl.BlockSpec((1, 1, past * A_HEADS, LANES), lambda b: (layer, b, 0, 0)),
                  pl.BlockSpec((4, A_DH), lambda b: (0, 0)),
                  pl.BlockSpec((1, LANES), lambda b: (0, 0))],
        out_specs=pl.BlockSpec((sq, A_WIDTH), row),
        out_shape=jax.ShapeDtypeStruct((t, A_WIDTH), BF16),
        compiler_params=_params(1), name="diff_attn_sample",
    )(q, kn, vn, cache_kt, cache_v, lam_p, sub)


def _pool_kernel(a_ref, u_ref, hist_ref, pw_ref, ps_ref, o_ref, ext_ref, *, n_hist):
    i = pl.program_id(1)
    tm = u_ref.shape[0]

    @pl.when(i == 0)
    def _():
        ext_ref[0:HIST_ROWS, :] = hist_ref[0]

    u = u_ref[...]
    ext_ref[HIST_ROWS:HIST_ROWS + tm, :] = u
    pos = (i * tm + lax.broadcasted_iota(jnp.int32, (tm, 1), 0)).astype(F32)
    o_ref[:, 0:A_WIDTH] = a_ref[...]
    for g, win in enumerate(POOL_WINDOWS):
        sl = slice(g * POOL_CH, (g + 1) * POOL_CH)
        run = ext_ref[:, sl]
        span = 1
        while span < win:
            run = run + pltpu.roll(run, span, 0)
            span *= 2
        tot = run[HIST_ROWS:]
        cnt = jnp.minimum(float(win), pos + (1.0 + n_hist))
        pooled = (tot / cnt - u[:, sl]).astype(BF16)
        y = jnp.dot(pooled, pw_ref[g], preferred_element_type=F32) * ps_ref[:, sl]
        o_ref[:, A_WIDTH + g * POOL_CH:A_WIDTH + (g + 1) * POOL_CH] = y.astype(BF16)
    ext_ref[0:HIST_ROWS, :] = ext_ref[tm:tm + HIST_ROWS, :]


def _pool(a, u, hist, pool_w, pool_scale, nseq, tm, n_hist):
    t = a.shape[0]
    nt = t // nseq // tm
    row = lambda b, i: (b * nt + i, 0)
    return pl.pallas_call(
        functools.partial(_pool_kernel, n_hist=n_hist),
        grid=(nseq, nt),
        in_specs=[pl.BlockSpec((tm, A_WIDTH), row), pl.BlockSpec((tm, B_WIDTH), row),
                  pl.BlockSpec((1, HIST_ROWS, B_WIDTH), lambda b, i: (b, 0, 0)),
                  pl.BlockSpec(pool_w.shape, lambda b, i: (0, 0, 0)),
                  pl.BlockSpec((1, B_WIDTH), lambda b, i: (0, 0))],
        out_specs=pl.BlockSpec((tm, D_MODEL), row),
        out_shape=jax.ShapeDtypeStruct((t, D_MODEL), BF16),
        scratch_shapes=[pltpu.VMEM((HIST_ROWS + tm, B_WIDTH), F32)],
        compiler_params=_params(2), name="pool_mix",
    )(a, u, hist, pool_w, pool_scale)


def _swa_bias(tq):
    nk, nq = 2 * tq, 4 * tq
    kc = (jnp.arange(2 * nk) % nk // CHUNK)[:, None]
    qc = (jnp.arange(nq) % tq // CHUNK)[None, :]
    band = (kc >= qc) & (kc <= qc + 2)
    return jnp.where(jnp.stack([band & (kc >= 2), band]), 0.0, -jnp.inf).astype(F32)


def _swa_prompt_kernel(sink_ref, q_ref, kp_ref, kc_ref, vp_ref, vc_ref, bias_ref, o_ref, sa_ref, sb_ref):
    tq = q_ref.shape[0]
    nk = 2 * tq
    nq = 4 * tq
    pair = lax.broadcasted_iota(jnp.int32, (1, nq), 1) // tq
    low_rows = lax.broadcasted_iota(jnp.int32, (LANES, 1), 0) < C_DH

    def scores(kv, s_ref):
        qs = jnp.concatenate([q_ref[:, (kv * 4 + p) * LANES:(kv * 4 + p + 1) * LANES] for p in range(4)], axis=0)
        ks = jnp.concatenate([kp_ref[2 * kv], kc_ref[2 * kv], kp_ref[2 * kv + 1], kc_ref[2 * kv + 1]], axis=0)
        s_ref[...] = lax.dot_general(ks, qs, NT_DIMS, preferred_element_type=F32)

    def attend(kv, s_ref):
        vt = jnp.concatenate([vp_ref[2 * kv], vc_ref[2 * kv], vp_ref[2 * kv + 1], vc_ref[2 * kv + 1]], axis=1)
        s = s_ref[...] + bias_ref[0]
        es, rdens = [], []
        for half in range(2):
            sink = jnp.zeros((1, nq), F32)
            for p in range(4):
                sink = jnp.where(pair == p, sink_ref[kv * 8 + 2 * p + half] * LOG2E, sink)
            sh = s[half * nk:(half + 1) * nk]
            m = jnp.maximum(jnp.max(sh, axis=0, keepdims=True), sink)
            e = jnp.exp2(sh - m)
            rdens.append(1.0 / (jnp.sum(e, axis=0, keepdims=True) + jnp.exp2(sink - m)))
            es.append(e.astype(BF16))
        o = jnp.dot(vt, jnp.concatenate(es, axis=0), preferred_element_type=F32)
        o = (o * jnp.where(low_rows, rdens[0], rdens[1])).T
        for pr in range(4):
            o_ref[:, (kv * 4 + pr) * LANES:(kv * 4 + pr + 1) * LANES] = o[pr * tq:(pr + 1) * tq].astype(BF16)

    scores(0, sa_ref)
    scores(1, sb_ref)
    attend(0, sa_ref)
    attend(1, sb_ref)


def _swa_prompt(sinks, q, kk, vvt, batch, seq):
    t = q.shape[0]
    tq = SWA_TILE
    nt = seq // tq
    kspec = lambda f: pl.BlockSpec((4, tq, LANES), f)
    vspec = lambda f: pl.BlockSpec((4, LANES, tq), f)
    return pl.pallas_call(
        _swa_prompt_kernel,
        grid=(batch, nt),
        in_specs=[pl.BlockSpec(memory_space=pltpu.SMEM),
                  pl.BlockSpec((tq, C_HEADS * C_DH), lambda b, i: (b * nt + i, 0)),
                  kspec(lambda b, i: (0, b * nt + jnp.maximum(i - 1, 0), 0)), kspec(lambda b, i: (0, b * nt + i, 0)),
                  vspec(lambda b, i: (0, 0, b * nt + jnp.maximum(i - 1, 0))), vspec(lambda b, i: (0, 0, b * nt + i)),
                  pl.BlockSpec((1, 4 * tq, 4 * tq), lambda b, i: (jnp.minimum(i, 1), 0, 0))],
        out_specs=pl.BlockSpec((tq, C_HEADS * C_DH), lambda b, i: (b * nt + i, 0)),
        out_shape=jax.ShapeDtypeStruct((t, C_HEADS * C_DH), BF16),
        scratch_shapes=[pltpu.VMEM((4 * tq, 4 * tq), F32), pltpu.VMEM((4 * tq, 4 * tq), F32)],
        compiler_params=_params(2), name="swa_prompt",
    )(sinks, q, kk, kk, vvt, vvt, _swa_bias(tq))


def _swa_sample_kernel(sink_ref, q_ref, kk_ref, vv_ref, ck_ref, cv_ref, o_ref):
    sq = q_ref.shape[0]
    nc = ck_ref.shape[1]
    lo = lax.broadcasted_iota(jnp.int32, (nc, LANES), 1) < C_DH

    def halves(c, kv):
        own = jnp.where(lo, c, 0.0) if kv == 0 else jnp.where(lo, 0.0, c)
        swapped = pltpu.roll(own, C_DH, 1)
        return ((own, swapped) if kv == 0 else (swapped, own))

    ck = ck_ref[0]
    cv = cv_ref[0]
    for kv in range(C_KV):
        qs = jnp.concatenate([q_ref[:, (kv * 4 + p) * LANES:(kv * 4 + p + 1) * LANES] for p in range(4)], axis=0)
        ckh = halves(ck, kv)
        cvh = halves(cv, kv)
        o = jnp.zeros((4 * sq, LANES), F32)
        for half in range(2):
            ks = jnp.concatenate([ckh[half].astype(BF16), kk_ref[2 * kv + half]], axis=0)
            vs = jnp.concatenate([cvh[half].astype(BF16), vv_ref[2 * kv + half]], axis=0)
            s = lax.dot_general(qs, ks, NT_DIMS, preferred_element_type=F32)
            sink = jnp.concatenate(
                [jnp.full((sq, 1), sink_ref[kv * 8 + 2 * p + half] * LOG2E, F32) for p in range(4)], axis=0)
            m = jnp.maximum(jnp.max(s, axis=-1, keepdims=True), sink)
            e = jnp.exp2(s - m)
            p = e / (jnp.sum(e, axis=-1, keepdims=True) + jnp.exp2(sink - m))
            o = o + jnp.dot(p.astype(BF16), vs, preferred_element_type=F32)
        for pr in range(4):
            o_ref[:, (kv * 4 + pr) * LANES:(kv * 4 + pr + 1) * LANES] = o[pr * sq:(pr + 1) * sq].astype(BF16)


def _swa_sample(sinks, q, kk, vv, cache_k, cache_v):
    nb, nc, _ = cache_k.shape
    t = q.shape[0]
    sq = t // nb
    return pl.pallas_call(
        _swa_sample_kernel,
        grid=(nb,),
        in_specs=[pl.BlockSpec(memory_space=pltpu.SMEM),
                  pl.BlockSpec((sq, C_HEADS * C_DH), lambda b: (b, 0)),
                  pl.BlockSpec((4, sq, LANES), lambda b: (0, b, 0)),
                  pl.BlockSpec((4, sq, LANES), lambda b: (0, b, 0)),
                  pl.BlockSpec((1, nc, LANES), lambda b: (b, 0, 0)),
                  pl.BlockSpec((1, nc, LANES), lambda b: (b, 0, 0))],
        out_specs=pl.BlockSpec((sq, C_HEADS * C_DH), lambda b: (b, 0)),
        out_shape=jax.ShapeDtypeStruct((t, C_HEADS * C_DH), BF16),
        compiler_params=_params(1), name="swa_sample",
    )(sinks, q, kk, vv, cache_k, cache_v)


def _shift_rows(up, prev, shift, nseq):
    seq_len = up.shape[0] // nseq
    rolled = pltpu.roll(up, shift, 0)
    first = lax.broadcasted_iota(jnp.int32, (SUBLANES, up.shape[1]), 0) < shift
    pieces = []
    for s in range(nseq):
        lo = s * seq_len
        head = jnp.where(first, pltpu.roll(prev[s], shift, 0), pltpu.roll(up[lo:lo + SUBLANES], shift, 0))
        pieces += [head, rolled[lo + SUBLANES:lo + seq_len]]
    return jnp.concatenate(pieces, axis=0)


def _ffn_kernel(x_ref, mix_ref, wo_ref, g_ref, hist_ref, wu_ref, cw_ref, cb_ref, wd_ref, gf_ref,
                y_ref, st_ref, hn_ref, h_ref, carry_ref, *, nseq, final_norm):
    i = pl.program_id(1)
    tm = x_ref.shape[0]
    seq_len = tm // nseq

    @pl.when(i == 0)
    def _():
        carry_ref[...] = hist_ref[0]

    x = x_ref[...] + jnp.dot(mix_ref[...], wo_ref[0], preferred_element_type=F32)
    hn_ref[...] = _rmsnorm_rows(x, g_ref[...]).astype(BF16)
    for c in range(N_FF_CHUNKS):
        conv = []
        for sl in (slice(c * FF_CHUNK, (c + 1) * FF_CHUNK), slice(D_FF + c * FF_CHUNK, D_FF + (c + 1) * FF_CHUNK)):
            up = jnp.dot(hn_ref[...], wu_ref[0, :, sl], preferred_element_type=F32)
            prev = [carry_ref[s, :, sl] for s in range(nseq)]
            conv.append(cb_ref[0, :, sl] + _shift_rows(up, prev, 2, nseq) * cw_ref[0, 0:1, sl]
                        + _shift_rows(up, prev, 1, nseq) * cw_ref[0, 1:2, sl] + up * cw_ref[0, 2:3, sl])
            for s in range(nseq):
                carry_ref[s, :, sl] = up[(s + 1) * seq_len - SUBLANES:(s + 1) * seq_len]
        gate, val = conv
        h_ref[:, c * FF_CHUNK:(c + 1) * FF_CHUNK] = (gate * jax.nn.sigmoid(gate) * val).astype(BF16)
    y = x + jnp.dot(h_ref[...], wd_ref[0], preferred_element_type=F32)
    if final_norm:
        y = _rmsnorm_rows(y, gf_ref[...])
    y_ref[...] = y
    st_ref[0] = carry_ref[...]


def _ffn(x, mix, wo, mix_layer, g, hist, wu, cw, cb, wd, layer, g_final, nseq, tm, final_norm):
    t = x.shape[0]
    n_outer = hist.shape[0]
    nt = t // n_outer // tm
    row = lambda b, i: (b * nt + i, 0)
    c2 = lambda b, i: (0, 0)
    lay = lambda b, i: (layer, 0, 0)
    once = pl.Buffered(1)
    st_block = (1,) + hist.shape[1:]
    st_map = lambda b, i: (b, 0, 0, 0)
    return pl.pallas_call(
        functools.partial(_ffn_kernel, nseq=nseq, final_norm=final_norm),
        grid=(n_outer, nt),
        in_specs=[pl.BlockSpec((tm, D_MODEL), row), pl.BlockSpec((tm, D_MODEL), row),
                  pl.BlockSpec((1,) + wo.shape[1:], lambda b, i: (mix_layer, 0, 0), pipeline_mode=once),
                  pl.BlockSpec((1, D_MODEL), c2),
                  pl.BlockSpec(st_block, st_map),
                  pl.BlockSpec((1,) + wu.shape[1:], lay, pipeline_mode=once), pl.BlockSpec((1,) + cw.shape[1:], lay),
                  pl.BlockSpec((1,) + cb.shape[1:], lay), pl.BlockSpec((1,) + wd.shape[1:], lay, pipeline_mode=once),
                  pl.BlockSpec((1, D_MODEL), c2)],
        out_specs=[pl.BlockSpec((tm, D_MODEL), row), pl.BlockSpec(st_block, st_map)],
        out_shape=[jax.ShapeDtypeStruct((t, D_MODEL), F32), jax.ShapeDtypeStruct(hist.shape, F32)],
        scratch_shapes=[pltpu.VMEM((tm, D_MODEL), BF16),
                        pltpu.VMEM((tm, D_FF), BF16),
                        pltpu.VMEM(hist.shape[1:], F32)],
        compiler_params=_params(2), name="conv_ffn",
    )(x, mix, wo, g, hist, wu, cw, cb, wd, g_final)


def kernel(x_prompt, x_sample, cache_diff_k, cache_diff_v, state_pool, cache_swa_k, cache_swa_v, state_ffn_conv,
           norm_attn, norm_ffn, norm_final, w_in_even, w_out_even, diff_lambda, diff_subln, pool_w, pool_scale,
           w_in_odd, b_in_odd, w_out_odd, sinks, w_up, conv_w, conv_b, w_down):
    bp, sp, d = x_prompt.shape
    bs, ss, _ = x_sample.shape
    depth = norm_attn.shape[0]
    past = cache_diff_k.shape[2]
    hp = x_prompt.reshape(bp * sp, d)
    hs = x_sample.reshape(bs * ss, d)
    tm_s = bs * ss

    tabs_p = _rope_tables(jnp.arange(sp, dtype=jnp.int32))
    tabs_s = tuple(jnp.tile(t, (bs, 1)) for t in _rope_tables(past + jnp.arange(ss, dtype=jnp.int32)))

    wu_b = w_up.astype(BF16)
    wo_even = w_out_even.astype(BF16)
    wo_odd = w_out_odd.astype(BF16)
    wd_b = w_down.astype(BF16)
    cb3 = conv_b[:, None, :]
    conv_pad = jnp.zeros((bs, SUBLANES - (CONV_W - 1), 2 * D_FF), F32)
    zero_conv = jnp.zeros((bp, 1, SUBLANES, 2 * D_FF), F32)
    zero_pool = jnp.zeros((bp, HIST_ROWS, B_WIDTH), F32)
    n_even = cache_diff_k.shape[0]
    cache_kt = jnp.transpose(cache_diff_k, (0, 1, 3, 4, 5, 2)).reshape(n_even, bs, A_WIDTH, past)
    cache_vr = cache_diff_v.reshape(n_even, bs, past * A_HEADS, 2 * A_DH)

    dkp, dvp, plp, skp, svp, fcp = [], [], [], [], [], []
    dks, dvs, pls, sks, svs, fcs = [], [], [], [], [], []
    for i in range(depth):
        j = i // 2
        g_attn = norm_attn[i][None, :]
        if i % 2 == 0:
            lam_init = 0.8 - 0.6 * math.exp(-0.3 * i)
            w_in = w_in_even[j].astype(BF16)
            pw = pool_w[j].astype(BF16)
            ps = pool_scale[j][None, :]
            lam_p = diff_lambda[j]
            wo = wo_even

            q, kt, kb, v, vt, u = _even_in(hp, g_attn, w_in, tabs_p, ROW_TILE, bp, True)
            a = _diff_prompt(q, kb, vt, lam_p, diff_subln[j][:, None], bp, sp, lam_init)
            mix_p = _pool(a, u, zero_pool, pw, ps, bp, ROW_TILE, 0)
            dkp.append(kt)
            dvp.append(v.reshape(bp, sp, A_HEADS, 2 * A_DH))
            plp.append(u.reshape(bp, sp, B_WIDTH)[:, sp - POOL_HIST:])

            q, k, kb, v, vb, u = _even_in(hs, g_attn, w_in, tabs_s, tm_s, 1, False)
            a = _diff_sample(q, kb, vb, cache_kt, cache_vr, j, lam_p, diff_subln[j][None, :], lam_init)
            hist = jnp.concatenate([jnp.zeros((bs, HIST_ROWS - POOL_HIST, B_WIDTH), F32), state_pool[j]], axis=1)
            mix_s = _pool(a, u, hist, pw, ps, bs, ss, POOL_HIST)
            dks.append(k.reshape(bs, ss, A_HEADS, 2, A_DH))
            dvs.append(v.reshape(bs, ss, A_HEADS, 2 * A_DH))
            pls.append(u.reshape(bs, ss, B_WIDTH)[:, ss - POOL_HIST:])
        else:
            w_in = w_in_odd[j].astype(BF16)
            b_in = b_in_odd[j][None, :]
            sk = sinks[j]
            wo = wo_odd

            q, k, v, kk, vvt = _odd_in(hp, g_attn, w_in, b_in, tabs_p, ROW_TILE, bp, C_CACHE, True)
            mix_p = _swa_prompt(sk, q, kk, vvt, bp, sp)
            skp.append(k.reshape(bp, C_CACHE, C_KV, C_DH))
            svp.append(v.reshape(bp, C_CACHE, C_KV, C_DH))

            q, k, v, kk, vv = _odd_in(hs, g_attn, w_in, b_in, tabs_s, tm_s, 1, tm_s, False)
            mix_s = _swa_sample(sk, q, kk, vv, cache_swa_k[j].reshape(bs, C_CACHE, C_KV * C_DH),
                                cache_swa_v[j].reshape(bs, C_CACHE, C_KV * C_DH))
            k_all = jnp.concatenate([cache_swa_k[j], k.reshape(bs, ss, C_KV, C_DH)], axis=1)
            v_all = jnp.concatenate([cache_swa_v[j], v.reshape(bs, ss, C_KV, C_DH)], axis=1)
            sks.append(k_all[:, -C_CACHE:])
            svs.append(v_all[:, -C_CACHE:])

        last = i == depth - 1
        g_ffn = norm_ffn[i][None, :]
        g_fin = norm_final[None, :]
        hp, st = _ffn(hp, mix_p, wo, j, g_ffn, zero_conv, wu_b, conv_w, cb3, wd_b, i, g_fin, 1, ROW_TILE, last)
        fcp.append(st[:, 0, SUBLANES - (CONV_W - 1):])
        hist = jnp.concatenate([conv_pad, state_ffn_conv[i]], axis=1)[None]
        hs, st = _ffn(hs, mix_s, wo, j, g_ffn, hist, wu_b, conv_w, cb3, wd_b, i, g_fin, bs, tm_s, last)
        fcs.append(st[0, :, SUBLANES - (CONV_W - 1):])

    diff_k_prompt = jnp.transpose(jnp.stack(dkp).reshape(n_even, bp, A_HEADS, 2, A_DH, sp), (0, 1, 5, 2, 3, 4))
    return (hp.reshape(bp, sp, d), hs.reshape(bs, ss, d),
            diff_k_prompt, jnp.stack(dvp), jnp.stack(plp), jnp.stack(skp), jnp.stack(svp), jnp.stack(fcp),
            jnp.stack(dks), jnp.stack(dvs), jnp.stack(pls), jnp.stack(sks), jnp.stack(svs), jnp.stack(fcs))
```

```python
import functools
import math

import jax
import jax.numpy as jnp
from jax import lax
from jax.experimental import pallas as pl
from jax.experimental.pallas import tpu as pltpu

F32 = jnp.float32
BF16 = jnp.bfloat16

D_MODEL = 1024
CHUNK = 64
ROPE_THETA = 10000.0
EPS = 1e-5
A_HEADS = 4
A_DH = 64
A_WIDTH = A_HEADS * 2 * A_DH
POOL_WINDOWS = (2, 4, 8, 16)
POOL_CH = 128
POOL_HIST = 15
B_WIDTH = 512
C_HEADS = 16
C_KV = 2
C_DH = 64
C_CACHE = 128
D_FF = 2816
CONV_W = 3

LOG2E = math.log2(math.e)
LANES = 128
SUBLANES = 8
HIST_ROWS = 16
FF_CHUNK = 256
N_FF_CHUNKS = D_FF // FF_CHUNK
ROW_TILE = 512
ATTN_TILE = 512
ATTN_QUERY_TILES = 4
SWA_TILE = 2 * CHUNK
VMEM_LIMIT = 56 * 1024 * 1024

NT_DIMS = (((1,), (1,)), ((), ()))


def _params(n_axes, vmem=VMEM_LIMIT):
    return pltpu.CompilerParams(dimension_semantics=("arbitrary",) * n_axes, vmem_limit_bytes=vmem)


def _rmsnorm_rows(x, g):
    return x * lax.rsqrt(jnp.mean(x * x, axis=-1, keepdims=True) + EPS) * g


def _rope128(z, cos, s_up, s_dn):
    return z * cos + pltpu.roll(z, 96, 1) * s_up + pltpu.roll(z, 32, 1) * s_dn


def _rope_tables(pos):
    inv = ROPE_THETA ** (-jnp.arange(0, A_DH, 2, dtype=F32) / A_DH)
    ang = pos.astype(F32)[:, None] * inv[None, :]
    cos, sin = jnp.cos(ang), jnp.sin(ang)
    zero = jnp.zeros_like(sin)
    cos128 = jnp.tile(cos, (1, 4))
    s_up = jnp.tile(jnp.concatenate([-sin, zero], axis=1), (1, 2))
    s_dn = jnp.tile(jnp.concatenate([zero, sin], axis=1), (1, 2))
    return cos128, s_up, s_dn


def _even_in_kernel(x_ref, g_ref, w_ref, cos_ref, sup_ref, sdn_ref,
                    q_ref, k_ref, kb_ref, v_ref, vx_ref, u_ref, *, prompt):
    tm = x_ref.shape[0]
    hn = _rmsnorm_rows(x_ref[...], g_ref[...]).astype(BF16)
    z = jnp.dot(hn, w_ref[...], preferred_element_type=F32)
    cos, s_up, s_dn = cos_ref[...], sup_ref[...], sdn_ref[...]
    for c in range(A_HEADS):
        sl = slice(c * LANES, (c + 1) * LANES)
        rq = _rope128(z[:, sl], cos, s_up, s_dn)
        q_ref[:, sl] = (rq * (A_DH ** -0.5 * LOG2E)).astype(BF16)
        rk = _rope128(z[:, A_WIDTH + c * LANES:A_WIDTH + (c + 1) * LANES], cos, s_up, s_dn)
        kb_ref[:, sl] = rk.astype(BF16)
        if prompt:
            k_ref[0, sl, :] = rk.T
        else:
            k_ref[:, sl] = rk
    v = z[:, 2 * A_WIDTH:3 * A_WIDTH]
    if prompt:
        for c in range(A_HEADS):
            v_ref[pl.ds(c, tm, stride=A_HEADS), :] = v[:, c * LANES:(c + 1) * LANES]
        vx_ref[0] = v.T.astype(BF16)
    else:
        v_ref[...] = v
        vx_ref[...] = v.astype(BF16)
    u_ref[...] = z[:, 3 * A_WIDTH:]


def _even_in(x, g, w, tabs, tm, nseq, prompt):
    t = x.shape[0]
    n_tab = tabs[0].shape[0] // tm
    nt = t // nseq // tm
    row = lambda i: (i, 0)
    tab = lambda i: (i % n_tab, 0)
    const = lambda i: (0, 0)
    if prompt:
        k_spec = pl.BlockSpec((1, A_WIDTH, tm), lambda i: (i // nt, 0, i % nt))
        k_shape = jax.ShapeDtypeStruct((nseq, A_WIDTH, t // nseq), F32)
        v_spec = pl.BlockSpec((tm * A_HEADS, LANES), row)
        v_shape = jax.ShapeDtypeStruct((t * A_HEADS, LANES), F32)
        vx_spec = pl.BlockSpec((1, A_WIDTH, tm), lambda i: (i, 0, 0))
        vx_shape = jax.ShapeDtypeStruct((t // tm, A_WIDTH, tm), BF16)
    else:
        k_spec = pl.BlockSpec((tm, A_WIDTH), row)
        k_shape = jax.ShapeDtypeStruct((t, A_WIDTH), F32)
        v_spec = pl.BlockSpec((tm, A_WIDTH), row)
        v_shape = jax.ShapeDtypeStruct((t, A_WIDTH), F32)
        vx_spec = pl.BlockSpec((tm, A_WIDTH), row)
        vx_shape = jax.ShapeDtypeStruct((t, A_WIDTH), BF16)
    return pl.pallas_call(
        functools.partial(_even_in_kernel, prompt=prompt),
        grid=(t // tm,),
        in_specs=[pl.BlockSpec((tm, D_MODEL), row), pl.BlockSpec((1, D_MODEL), const),
                  pl.BlockSpec(w.shape, const),
                  pl.BlockSpec((tm, LANES), tab), pl.BlockSpec((tm, LANES), tab), pl.BlockSpec((tm, LANES), tab)],
        out_specs=[pl.BlockSpec((tm, A_WIDTH), row), k_spec, pl.BlockSpec((tm, A_WIDTH), row),
                   v_spec, vx_spec,
                   pl.BlockSpec((tm, B_WIDTH), row)],
        out_shape=[jax.ShapeDtypeStruct((t, A_WIDTH), BF16), k_shape, jax.ShapeDtypeStruct((t, A_WIDTH), BF16),
                   v_shape, vx_shape,
                   jax.ShapeDtypeStruct((t, B_WIDTH), F32)],
        compiler_params=_params(1), name="even_in",
    )(x, g, w, *tabs)


def _odd_in_kernel(x_ref, g_ref, w_ref, b_ref, cos_ref, sup_ref, sdn_ref,
                   q_ref, k_ref, v_ref, kk_ref, vv_ref, *, transpose_v):
    tm = x_ref.shape[0]
    hn = _rmsnorm_rows(x_ref[...], g_ref[...]).astype(BF16)
    z = jnp.dot(hn, w_ref[...], preferred_element_type=F32) + b_ref[...]
    cos, s_up, s_dn = cos_ref[...], sup_ref[...], sdn_ref[...]
    nq = C_HEADS * C_DH
    for c in range(nq // LANES):
        sl = slice(c * LANES, (c + 1) * LANES)
        q_ref[:, sl] = (_rope128(z[:, sl], cos, s_up, s_dn) * (C_DH ** -0.5 * LOG2E)).astype(BF16)
    k = _rope128(z[:, nq:nq + LANES], cos, s_up, s_dn)
    v = z[:, nq + LANES:nq + 2 * LANES]
    tail = k_ref.shape[0]
    k_ref[...] = k[tm - tail:]
    v_ref[...] = v[tm - tail:]
    lo = lax.broadcasted_iota(jnp.int32, (tm, LANES), 1) < C_DH
    for src, dst in ((k, kk_ref),) if transpose_v else ((k, kk_ref), (v, vv_ref)):
        h0 = jnp.where(lo, src, 0.0)
        h1 = jnp.where(lo, 0.0, src)
        for n, val in enumerate((h0, pltpu.roll(h0, C_DH, 1), pltpu.roll(h1, C_DH, 1), h1)):
            dst[n] = val.astype(BF16)
    if transpose_v:
        vt = v.T.astype(BF16)
        zero = jnp.zeros((C_DH, tm), BF16)
        vv_ref[0] = jnp.concatenate([vt[:C_DH], zero], axis=0)
        vv_ref[1] = jnp.concatenate([zero, vt[:C_DH]], axis=0)
        vv_ref[2] = jnp.concatenate([vt[C_DH:], zero], axis=0)
        vv_ref[3] = jnp.concatenate([zero, vt[C_DH:]], axis=0)


def _odd_in(x, g, w, b, tabs, tm, nseq, tail, transpose_v):
    t = x.shape[0]
    n_tab = tabs[0].shape[0] // tm
    nt = t // nseq // tm
    nq = C_HEADS * C_DH
    row = lambda i: (i, 0)
    kv_spec = pl.BlockSpec((tail, LANES), lambda i: (i // nt, 0))
    kv_shape = jax.ShapeDtypeStruct((nseq * tail, LANES), F32)
    tab = lambda i: (i % n_tab, 0)
    const = lambda i: (0, 0)
    if transpose_v:
        vv_spec = pl.BlockSpec((4, LANES, tm), lambda i: (0, 0, i))
        vv_shape = jax.ShapeDtypeStruct((4, LANES, t), BF16)
    else:
        vv_spec = pl.BlockSpec((4, tm, LANES), lambda i: (0, i, 0))
        vv_shape = jax.ShapeDtypeStruct((4, t, LANES), BF16)
    return pl.pallas_call(
        functools.partial(_odd_in_kernel, transpose_v=transpose_v),
        grid=(t // tm,),
        in_specs=[pl.BlockSpec((tm, D_MODEL), row), pl.BlockSpec((1, D_MODEL), const),
                  pl.BlockSpec(w.shape, const), pl.BlockSpec((1, w.shape[1]), const),
                  pl.BlockSpec((tm, LANES), tab), pl.BlockSpec((tm, LANES), tab), pl.BlockSpec((tm, LANES), tab)],
        out_specs=[pl.BlockSpec((tm, nq), row), kv_spec, kv_spec,
                   pl.BlockSpec((4, tm, LANES), lambda i: (0, i, 0)), vv_spec],
        out_shape=[jax.ShapeDtypeStruct((t, nq), BF16), kv_shape, kv_shape,
                   jax.ShapeDtypeStruct((4, t, LANES), BF16), vv_shape],
        compiler_params=_params(1), name="odd_in",
    )(x, g, w, b, *tabs)


def _diff_lambda(lam_ref, lam_init):
    lp = lam_ref[...]
    return (jnp.exp(jnp.sum(lp[0:1] * lp[1:2], axis=-1, keepdims=True))
            - jnp.exp(jnp.sum(lp[2:3] * lp[3:4], axis=-1, keepdims=True)) + lam_init)


def _diff_prompt_kernel(q_ref, kb_ref, vt_ref, lam_ref, sub_ref, o_ref, m_ref, l_ref, acc_ref, sa_ref, sb_ref,
                        *, lam_init):
    i = pl.program_id(2)
    tq = q_ref.shape[0]
    tk = vt_ref.shape[2]
    q = q_ref[...].astype(F32)
    lo = lax.broadcasted_iota(jnp.int32, (tq, LANES), 1) < A_DH
    qm = (jnp.where(lo, q, 0.0).astype(BF16), jnp.where(lo, 0.0, q).astype(BF16))
    m_ref[...] = jnp.full(m_ref.shape, -jnp.inf, F32)
    l_ref[...] = jnp.zeros(l_ref.shape, F32)
    acc_ref[...] = jnp.zeros(acc_ref.shape, F32)

    def scores(j, s_ref, cols):
        start = pl.multiple_of(j * tk, tk)
        k = kb_ref[pl.ds(start, tk), :]
        for mp in range(2):
            s_ref[mp, :, cols] = lax.dot_general(k, qm[mp][cols], NT_DIMS, preferred_element_type=F32)

    def consume(j, s_ref, mask, cols):
        vt = vt_ref[j]
        for mp in range(2):
            s = s_ref[mp, :, cols]
            if mask is not None:
                s = jnp.where(mask, s, -jnp.inf)
            m_prev = m_ref[mp, :, cols]
            m_new = jnp.maximum(m_prev, jnp.max(s, axis=0, keepdims=True))
            alpha = jnp.exp2(m_prev - m_new)
            p = jnp.exp2(s - m_new)
            l_ref[mp, :, cols] = alpha * l_ref[mp, :, cols] + jnp.sum(p, axis=0, keepdims=True)
            acc_ref[mp, :, cols] = (alpha * acc_ref[mp, :, cols]
                                    + jnp.dot(vt, p.astype(BF16), preferred_element_type=F32))
            m_ref[mp, :, cols] = m_new

    every = slice(0, tq)
    reps = tq // tk
    assert reps % 4 == 0
    scores(0, sa_ref, every)

    def pair(j0):
        scores(j0 + 1, sb_ref, every)
        consume(j0, sa_ref, None, every)
        scores(j0 + 2, sa_ref, every)
        consume(j0 + 1, sb_ref, None, every)

    def quad(t, carry):
        pair(4 * t)
        pair(4 * t + 2)
        return carry

    lax.fori_loop(0, i * (reps // 4), quad, 0)
    k_chunk = lax.broadcasted_iota(jnp.int32, (tk, tq), 0) // CHUNK
    q_chunk = lax.broadcasted_iota(jnp.int32, (tk, tq), 1) // CHUNK
    causal = k_chunk <= q_chunk
    bufs = (sa_ref, sb_ref)
    for r in range(reps):
        if r + 1 < reps:
            scores(reps * i + r + 1, bufs[(r + 1) % 2], slice((r + 1) * tk, tq))
        consume(reps * i + r, bufs[r % 2], causal[:, :tq - r * tk], slice(r * tk, tq))

    lam = _diff_lambda(lam_ref, lam_init)
    a = acc_ref[0] / l_ref[0] - lam * (acc_ref[1] / l_ref[1])
    a = a * lax.rsqrt(jnp.mean(a * a, axis=0, keepdims=True) + EPS) * sub_ref[...] * (1.0 - lam_init)
    o_ref[...] = a.T.astype(BF16)


def _diff_prompt(q, kb, vt, lam_p, sub_col, batch, seq, lam_init):
    t = q.shape[0]
    tk = ATTN_TILE
    tq = ATTN_QUERY_TILES * tk
    nq = seq // tq
    assert vt.shape == (t // tk, A_WIDTH, tk)
    return pl.pallas_call(
        functools.partial(_diff_prompt_kernel, lam_init=lam_init),
        grid=(batch, A_HEADS, nq),
        in_specs=[pl.BlockSpec((tq, LANES), lambda b, h, i: (b * nq + i, h)),
                  pl.BlockSpec((seq, LANES), lambda b, h, i: (b, h)),
                  pl.BlockSpec((seq // tk, LANES, tk), lambda b, h, i: (b, h, 0)),
                  pl.BlockSpec((4, A_DH), lambda b, h, i: (0, 0)),
                  pl.BlockSpec((LANES, 1), lambda b, h, i: (0, 0))],
        out_specs=pl.BlockSpec((tq, LANES), lambda b, h, i: (b * nq + i, h)),
        out_shape=jax.ShapeDtypeStruct((t, A_WIDTH), BF16),
        scratch_shapes=[pltpu.VMEM((2, 1, tq), F32), pltpu.VMEM((2, 1, tq), F32),
                        pltpu.VMEM((2, LANES, tq), F32),
                        pltpu.VMEM((2, tk, tq), F32), pltpu.VMEM((2, tk, tq), F32)],
        compiler_params=_params(3), name="diff_attn_prompt",
    )(q, kb, vt, lam_p, sub_col)


def _diff_sample_kernel(q_ref, kn_ref, vn_ref, ckt_ref, cv_ref, lam_ref, sub_ref, o_ref, *, lam_init):
    sq = q_ref.shape[0]
    past = ckt_ref.shape[3]
    lo = lax.broadcasted_iota(jnp.int32, (sq, LANES), 1) < A_DH
    lam = _diff_lambda(lam_ref, lam_init)
    for h in range(A_HEADS):
        sl = slice(h * LANES, (h + 1) * LANES)
        q = q_ref[:, sl].astype(F32)
        q2 = jnp.concatenate([jnp.where(lo, q, 0.0), jnp.where(lo, 0.0, q)], axis=0).astype(BF16)
        s_c = jnp.dot(q2, ckt_ref[0, 0, sl, :].astype(BF16), preferred_element_type=F32)
        s_n = lax.dot_general(q2, kn_ref[:, sl], NT_DIMS, preferred_element_type=F32)
        m = jnp.maximum(jnp.max(s_c, axis=-1, keepdims=True), jnp.max(s_n, axis=-1, keepdims=True))
        e_c = jnp.exp2(s_c - m)
        e_n = jnp.exp2(s_n - m)
        den = jnp.sum(e_c, axis=-1, keepdims=True) + jnp.sum(e_n, axis=-1, keepdims=True)
        cv = cv_ref[0, 0, pl.ds(h, past, stride=A_HEADS), :].astype(BF16)
        o = (jnp.dot(e_c.astype(BF16), cv, preferred_element_type=F32)
             + jnp.dot(e_n.astype(BF16), vn_ref[:, sl], preferred_element_type=F32)) / den
        a = o[:sq] - lam * o[sq:]
        o_ref[:, sl] = (_rmsnorm_rows(a, sub_ref[...]) * (1.0 - lam_init)).astype(BF16)


def _diff_sample(q, kn, vn, cache_kt, cache_v, layer, lam_p, sub, lam_init):
    _, nb, _, past = cache_kt.shape
    t = q.shape[0]
    sq = t // nb
    row = lambda b: (b, 0)
    return pl.pallas_call(
        functools.partial(_diff_sample_kernel, lam_init=lam_init),
        grid=(nb,),
        in_specs=[pl.BlockSpec((sq, A_WIDTH), row), pl.BlockSpec((sq, A_WIDTH), row), pl.BlockSpec((sq, A_WIDTH), row),
                  pl.BlockSpec((1, 1, A_WIDTH, past), lambda b: (layer, b, 0, 0)),
                  pl.BlockSpec((1, 1, past * A_HEADS, LANES), lambda b: (layer, b, 0, 0)),
                  pl.BlockSpec((4, A_DH), lambda b: (0, 0)),
                  pl.BlockSpec((1, LANES), lambda b: (0, 0))],
        out_specs=pl.BlockSpec((sq, A_WIDTH), row),
        out_shape=jax.ShapeDtypeStruct((t, A_WIDTH), BF16),
        compiler_params=_params(1), name="diff_attn_sample",
    )(q, kn, vn, cache_kt, cache_v, lam_p, sub)


def _pool_kernel(a_ref, u_ref, hist_ref, pw_ref, ps_ref, o_ref, ext_ref, *, n_hist):
    i = pl.program_id(1)
    tm = u_ref.shape[0]

    @pl.when(i == 0)
    def _():
        ext_ref[0:HIST_ROWS, :] = hist_ref[0]

    u = u_ref[...]
    ext_ref[HIST_ROWS:HIST_ROWS + tm, :] = u
    pos = (i * tm + lax.broadcasted_iota(jnp.int32, (tm, 1), 0)).astype(F32)
    o_ref[:, 0:A_WIDTH] = a_ref[...]
    for g, win in enumerate(POOL_WINDOWS):
        sl = slice(g * POOL_CH, (g + 1) * POOL_CH)
        run = ext_ref[:, sl]
        span = 1
        while span < win:
            run = run + pltpu.roll(run, span, 0)
            span *= 2
        tot = run[HIST_ROWS:]
        cnt = jnp.minimum(float(win), pos + (1.0 + n_hist))
        pooled = (tot / cnt - u[:, sl]).astype(BF16)
        y = jnp.dot(pooled, pw_ref[g], preferred_element_type=F32) * ps_ref[:, sl]
        o_ref[:, A_WIDTH + g * POOL_CH:A_WIDTH + (g + 1) * POOL_CH] = y.astype(BF16)
    ext_ref[0:HIST_ROWS, :] = ext_ref[tm:tm + HIST_ROWS, :]


def _pool(a, u, hist, pool_w, pool_scale, nseq, tm, n_hist):
    t = a.shape[0]
    nt = t // nseq // tm
    row = lambda b, i: (b * nt + i, 0)
    return pl.pallas_call(
        functools.partial(_pool_kernel, n_hist=n_hist),
        grid=(nseq, nt),
        in_specs=[pl.BlockSpec((tm, A_WIDTH), row), pl.BlockSpec((tm, B_WIDTH), row),
                  pl.BlockSpec((1, HIST_ROWS, B_WIDTH), lambda b, i: (b, 0, 0)),
                  pl.BlockSpec(pool_w.shape, lambda b, i: (0, 0, 0)),
                  pl.BlockSpec((1, B_WIDTH), lambda b, i: (0, 0))],
        out_specs=pl.BlockSpec((tm, D_MODEL), row),
        out_shape=jax.ShapeDtypeStruct((t, D_MODEL), BF16),
        scratch_shapes=[pltpu.VMEM((HIST_ROWS + tm, B_WIDTH), F32)],
        compiler_params=_params(2), name="pool_mix",
    )(a, u, hist, pool_w, pool_scale)


def _swa_bias(tq):
    nk, nq = 2 * tq, 4 * tq
    kc = (jnp.arange(2 * nk) % nk // CHUNK)[:, None]
    qc = (jnp.arange(nq) % tq // CHUNK)[None, :]
    band = (kc >= qc) & (kc <= qc + 2)
    return jnp.where(jnp.stack([band & (kc >= 2), band]), 0.0, -jnp.inf).astype(F32)


def _swa_prompt_kernel(sink_ref, q_ref, kp_ref, kc_ref, vp_ref, vc_ref, bias_ref, o_ref, sa_ref, sb_ref):
    tq = q_ref.shape[0]
    nk = 2 * tq
    nq = 4 * tq
    pair = lax.broadcasted_iota(jnp.int32, (1, nq), 1) // tq
    low_rows = lax.broadcasted_iota(jnp.int32, (LANES, 1), 0) < C_DH

    def scores(kv, s_ref):
        qs = jnp.concatenate([q_ref[:, (kv * 4 + p) * LANES:(kv * 4 + p + 1) * LANES] for p in range(4)], axis=0)
        ks = jnp.concatenate([kp_ref[2 * kv], kc_ref[2 * kv], kp_ref[2 * kv + 1], kc_ref[2 * kv + 1]], axis=0)
        s_ref[...] = lax.dot_general(ks, qs, NT_DIMS, preferred_element_type=F32)

    def attend(kv, s_ref):
        vt = jnp.concatenate([vp_ref[2 * kv], vc_ref[2 * kv], vp_ref[2 * kv + 1], vc_ref[2 * kv + 1]], axis=1)
        s = s_ref[...] + bias_ref[0]
        es, rdens = [], []
        for half in range(2):
            sink = jnp.zeros((1, nq), F32)
            for p in range(4):
                sink = jnp.where(pair == p, sink_ref[kv * 8 + 2 * p + half] * LOG2E, sink)
            sh = s[half * nk:(half + 1) * nk]
            m = jnp.maximum(jnp.max(sh, axis=0, keepdims=True), sink)
            e = jnp.exp2(sh - m)
            rdens.append(1.0 / (jnp.sum(e, axis=0, keepdims=True) + jnp.exp2(sink - m)))
            es.append(e.astype(BF16))
        o = jnp.dot(vt, jnp.concatenate(es, axis=0), preferred_element_type=F32)
        o = (o * jnp.where(low_rows, rdens[0], rdens[1])).T
        for pr in range(4):
            o_ref[:, (kv * 4 + pr) * LANES:(kv * 4 + pr + 1) * LANES] = o[pr * tq:(pr + 1) * tq].astype(BF16)

    scores(0, sa_ref)
    scores(1, sb_ref)
    attend(0, sa_ref)
    attend(1, sb_ref)


def _swa_prompt(sinks, q, kk, vvt, batch, seq):
    t = q.shape[0]
    tq = SWA_TILE
    nt = seq // tq
    kspec = lambda f: pl.BlockSpec((4, tq, LANES), f)
    vspec = lambda f: pl.BlockSpec((4, LANES, tq), f)
    return pl.pallas_call(
        _swa_prompt_kernel,
        grid=(batch, nt),
        in_specs=[pl.BlockSpec(memory_space=pltpu.SMEM),
                  pl.BlockSpec((tq, C_HEADS * C_DH), lambda b, i: (b * nt + i, 0)),
                  kspec(lambda b, i: (0, b * nt + jnp.maximum(i - 1, 0), 0)), kspec(lambda b, i: (0, b * nt + i, 0)),
                  vspec(lambda b, i: (0, 0, b * nt + jnp.maximum(i - 1, 0))), vspec(lambda b, i: (0, 0, b * nt + i)),
                  pl.BlockSpec((1, 4 * tq, 4 * tq), lambda b, i: (jnp.minimum(i, 1), 0, 0))],
        out_specs=pl.BlockSpec((tq, C_HEADS * C_DH), lambda b, i: (b * nt + i, 0)),
        out_shape=jax.ShapeDtypeStruct((t, C_HEADS * C_DH), BF16),
        scratch_shapes=[pltpu.VMEM((4 * tq, 4 * tq), F32), pltpu.VMEM((4 * tq, 4 * tq), F32)],
        compiler_params=_params(2), name="swa_prompt",
    )(sinks, q, kk, kk, vvt, vvt, _swa_bias(tq))


def _swa_sample_kernel(sink_ref, q_ref, kk_ref, vv_ref, ck_ref, cv_ref, o_ref):
    sq = q_ref.shape[0]
    nc = ck_ref.shape[1]
    lo = lax.broadcasted_iota(jnp.int32, (nc, LANES), 1) < C_DH

    def halves(c, kv):
        own = jnp.where(lo, c, 0.0) if kv == 0 else jnp.where(lo, 0.0, c)
        swapped = pltpu.roll(own, C_DH, 1)
        return ((own, swapped) if kv == 0 else (swapped, own))

    ck = ck_ref[0]
    cv = cv_ref[0]
    for kv in range(C_KV):
        qs = jnp.concatenate([q_ref[:, (kv * 4 + p) * LANES:(kv * 4 + p + 1) * LANES] for p in range(4)], axis=0)
        ckh = halves(ck, kv)
        cvh = halves(cv, kv)
        o = jnp.zeros((4 * sq, LANES), F32)
        for half in range(2):
            ks = jnp.concatenate([ckh[half].astype(BF16), kk_ref[2 * kv + half]], axis=0)
            vs = jnp.concatenate([cvh[half].astype(BF16), vv_ref[2 * kv + half]], axis=0)
            s = lax.dot_general(qs, ks, NT_DIMS, preferred_element_type=F32)
            sink = jnp.concatenate(
                [jnp.full((sq, 1), sink_ref[kv * 8 + 2 * p + half] * LOG2E, F32) for p in range(4)], axis=0)
            m = jnp.maximum(jnp.max(s, axis=-1, keepdims=True), sink)
            e = jnp.exp2(s - m)
            p = e / (jnp.sum(e, axis=-1, keepdims=True) + jnp.exp2(sink - m))
            o = o + jnp.dot(p.astype(BF16), vs, preferred_element_type=F32)
        for pr in range(4):
            o_ref[:, (kv * 4 + pr) * LANES:(kv * 4 + pr + 1) * LANES] = o[pr * sq:(pr + 1) * sq].astype(BF16)


def _swa_sample(sinks, q, kk, vv, cache_k, cache_v):
    nb, nc, _ = cache_k.shape
    t = q.shape[0]
    sq = t // nb
    return pl.pallas_call(
        _swa_sample_kernel,
        grid=(nb,),
        in_specs=[pl.BlockSpec(memory_space=pltpu.SMEM),
                  pl.BlockSpec((sq, C_HEADS * C_DH), lambda b: (b, 0)),
                  pl.BlockSpec((4, sq, LANES), lambda b: (0, b, 0)),
                  pl.BlockSpec((4, sq, LANES), lambda b: (0, b, 0)),
                  pl.BlockSpec((1, nc, LANES), lambda b: (b, 0, 0)),
                  pl.BlockSpec((1, nc, LANES), lambda b: (b, 0, 0))],
        out_specs=pl.BlockSpec((sq, C_HEADS * C_DH), lambda b: (b, 0)),
        out_shape=jax.ShapeDtypeStruct((t, C_HEADS * C_DH), BF16),
        compiler_params=_params(1), name="swa_sample",
    )(sinks, q, kk, vv, cache_k, cache_v)


def _shift_rows(up, prev, shift, nseq):
    seq_len = up.shape[0] // nseq
    rolled = pltpu.roll(up, shift, 0)
    first = lax.broadcasted_iota(jnp.int32, (SUBLANES, up.shape[1]), 0) < shift
    pieces = []
    for s in range(nseq):
        lo = s * seq_len
        head = jnp.where(first, pltpu.roll(prev[s], shift, 0), pltpu.roll(up[lo:lo + SUBLANES], shift, 0))
        pieces += [head, rolled[lo + SUBLANES:lo + seq_len]]
    return jnp.concatenate(pieces, axis=0)


def _ffn_kernel(x_ref, mix_ref, wo_ref, g_ref, hist_ref, wu_ref, cw_ref, cb_ref, wd_ref, gf_ref,
                y_ref, st_ref, hn_ref, h_ref, carry_ref, *, nseq, final_norm):
    i = pl.program_id(1)
    tm = x_ref.shape[0]
    seq_len = tm // nseq

    @pl.when(i == 0)
    def _():
        carry_ref[...] = hist_ref[0]

    x = x_ref[...] + jnp.dot(mix_ref[...], wo_ref[0], preferred_element_type=F32)
    hn_ref[...] = _rmsnorm_rows(x, g_ref[...]).astype(BF16)
    for c in range(N_FF_CHUNKS):
        conv = []
        for sl in (slice(c * FF_CHUNK, (c + 1) * FF_CHUNK), slice(D_FF + c * FF_CHUNK, D_FF + (c + 1) * FF_CHUNK)):
            up = jnp.dot(hn_ref[...], wu_ref[0, :, sl], preferred_element_type=F32)
            prev = [carry_ref[s, :, sl] for s in range(nseq)]
            conv.append(cb_ref[0, :, sl] + _shift_rows(up, prev, 2, nseq) * cw_ref[0, 0:1, sl]
                        + _shift_rows(up, prev, 1, nseq) * cw_ref[0, 1:2, sl] + up * cw_ref[0, 2:3, sl])
            for s in range(nseq):
                carry_ref[s, :, sl] = up[(s + 1) * seq_len - SUBLANES:(s + 1) * seq_len]
        gate, val = conv
        h_ref[:, c * FF_CHUNK:(c + 1) * FF_CHUNK] = (gate * jax.nn.sigmoid(gate) * val).astype(BF16)
    y = x + jnp.dot(h_ref[...], wd_ref[0], preferred_element_type=F32)
    if final_norm:
        y = _rmsnorm_rows(y, gf_ref[...])
    y_ref[...] = y
    st_ref[0] = carry_ref[...]


def _ffn(x, mix, wo, mix_layer, g, hist, wu, cw, cb, wd, layer, g_final, nseq, tm, final_norm):
    t = x.shape[0]
    n_outer = hist.shape[0]
    nt = t // n_outer // tm
    row = lambda b, i: (b * nt + i, 0)
    c2 = lambda b, i: (0, 0)
    lay = lambda b, i: (layer, 0, 0)
    once = pl.Buffered(1)
    st_block = (1,) + hist.shape[1:]
    st_map = lambda b, i: (b, 0, 0, 0)
    return pl.pallas_call(
        functools.partial(_ffn_kernel, nseq=nseq, final_norm=final_norm),
        grid=(n_outer, nt),
        in_specs=[pl.BlockSpec((tm, D_MODEL), row), pl.BlockSpec((tm, D_MODEL), row),
                  pl.BlockSpec((1,) + wo.shape[1:], lambda b, i: (mix_layer, 0, 0), pipeline_mode=once),
                  pl.BlockSpec((1, D_MODEL), c2),
                  pl.BlockSpec(st_block, st_map),
                  pl.BlockSpec((1,) + wu.shape[1:], lay, pipeline_mode=once), pl.BlockSpec((1,) + cw.shape[1:], lay),
                  pl.BlockSpec((1,) + cb.shape[1:], lay), pl.BlockSpec((1,) + wd.shape[1:], lay, pipeline_mode=once),
                  pl.BlockSpec((1, D_MODEL), c2)],
        out_specs=[pl.BlockSpec((tm, D_MODEL), row), pl.BlockSpec(st_block, st_map)],
        out_shape=[jax.ShapeDtypeStruct((t, D_MODEL), F32), jax.ShapeDtypeStruct(hist.shape, F32)],
        scratch_shapes=[pltpu.VMEM((tm, D_MODEL), BF16),
                        pltpu.VMEM((tm, D_FF), BF16),
                        pltpu.VMEM(hist.shape[1:], F32)],
        compiler_params=_params(2), name="conv_ffn",
    )(x, mix, wo, g, hist, wu, cw, cb, wd, g_final)


def kernel(x_prompt, x_sample, cache_diff_k, cache_diff_v, state_pool, cache_swa_k, cache_swa_v, state_ffn_conv,
           norm_attn, norm_ffn, norm_final, w_in_even, w_out_even, diff_lambda, diff_subln, pool_w, pool_scale,
           w_in_odd, b_in_odd, w_out_odd, sinks, w_up, conv_w, conv_b, w_down):
    bp, sp, d = x_prompt.shape
    bs, ss, _ = x_sample.shape
    depth = norm_attn.shape[0]
    past = cache_diff_k.shape[2]
    hp = x_prompt.reshape(bp * sp, d)
    hs = x_sample.reshape(bs * ss, d)
    tm_s = bs * ss

    tabs_p = _rope_tables(jnp.arange(sp, dtype=jnp.int32))
    tabs_s = tuple(jnp.tile(t, (bs, 1)) for t in _rope_tables(past + jnp.arange(ss, dtype=jnp.int32)))

    wu_b = w_up.astype(BF16)
    wo_even = w_out_even.astype(BF16)
    wo_odd = w_out_odd.astype(BF16)
    wd_b = w_down.astype(BF16)
    cb3 = conv_b[:, None, :]
    conv_pad = jnp.zeros((bs, SUBLANES - (CONV_W - 1), 2 * D_FF), F32)
    zero_conv = jnp.zeros((bp, 1, SUBLANES, 2 * D_FF), F32)
    zero_pool = jnp.zeros((bp, HIST_ROWS, B_WIDTH), F32)
    n_even = cache_diff_k.shape[0]
    cache_kt = jnp.transpose(cache_diff_k, (0, 1, 3, 4, 5, 2)).reshape(n_even, bs, A_WIDTH, past)
    cache_vr = cache_diff_v.reshape(n_even, bs, past * A_HEADS, 2 * A_DH)

    dkp, dvp, plp, skp, svp, fcp = [], [], [], [], [], []
    dks, dvs, pls, sks, svs, fcs = [], [], [], [], [], []
    for i in range(depth):
        j = i // 2
        g_attn = norm_attn[i][None, :]
        if i % 2 == 0:
            lam_init = 0.8 - 0.6 * math.exp(-0.3 * i)
            w_in = w_in_even[j].astype(BF16)
            pw = pool_w[j].astype(BF16)
            ps = pool_scale[j][None, :]
            lam_p = diff_lambda[j]
            wo = wo_even

            q, kt, kb, v, vt, u = _even_in(hp, g_attn, w_in, tabs_p, ROW_TILE, bp, True)
            a = _diff_prompt(q, kb, vt, lam_p, diff_subln[j][:, None], bp, sp, lam_init)
            mix_p = _pool(a, u, zero_pool, pw, ps, bp, ROW_TILE, 0)
            dkp.append(kt)
            dvp.append(v.reshape(bp, sp, A_HEADS, 2 * A_DH))
            plp.append(u.reshape(bp, sp, B_WIDTH)[:, sp - POOL_HIST:])

            q, k, kb, v, vb, u = _even_in(hs, g_attn, w_in, tabs_s, tm_s, 1, False)
            a = _diff_sample(q, kb, vb, cache_kt, cache_vr, j, lam_p, diff_subln[j][None, :], lam_init)
            hist = jnp.concatenate([jnp.zeros((bs, HIST_ROWS - POOL_HIST, B_WIDTH), F32), state_pool[j]], axis=1)
            mix_s = _pool(a, u, hist, pw, ps, bs, ss, POOL_HIST)
            dks.append(k.reshape(bs, ss, A_HEADS, 2, A_DH))
            dvs.append(v.reshape(bs, ss, A_HEADS, 2 * A_DH))
            pls.append(u.reshape(bs, ss, B_WIDTH)[:, ss - POOL_HIST:])
        else:
            w_in = w_in_odd[j].astype(BF16)
            b_in = b_in_odd[j][None, :]
            sk = sinks[j]
            wo = wo_odd

            q, k, v, kk, vvt = _odd_in(hp, g_attn, w_in, b_in, tabs_p, ROW_TILE, bp, C_CACHE, True)
            mix_p = _swa_prompt(sk, q, kk, vvt, bp, sp)
            skp.append(k.reshape(bp, C_CACHE, C_KV, C_DH))
            svp.append(v.reshape(bp, C_CACHE, C_KV, C_DH))

            q, k, v, kk, vv = _odd_in(hs, g_attn, w_in, b_in, tabs_s, tm_s, 1, tm_s, False)
            mix_s = _swa_sample(sk, q, kk, vv, cache_swa_k[j].reshape(bs, C_CACHE, C_KV * C_DH),
                                cache_swa_v[j].reshape(bs, C_CACHE, C_KV * C_DH))
            k_all = jnp.concatenate([cache_swa_k[j], k.reshape(bs, ss, C_KV, C_DH)], axis=1)
            v_all = jnp.concatenate([cache_swa_v[j], v.reshape(bs, ss, C_KV, C_DH)], axis=1)
            sks.append(k_all[:, -C_CACHE:])
            svs.append(v_all[:, -C_CACHE:])

        last = i == depth - 1
        g_ffn = norm_ffn[i][None, :]
        g_fin = norm_final[None, :]
        hp, st = _ffn(hp, mix_p, wo, j, g_ffn, zero_conv, wu_b, conv_w, cb3, wd_b, i, g_fin, 1, ROW_TILE, last)
        fcp.append(st[:, 0, SUBLANES - (CONV_W - 1):])
        hist = jnp.concatenate([conv_pad, state_ffn_conv[i]], axis=1)[None]
        hs, st = _ffn(hs, mix_s, wo, j, g_ffn, hist, wu_b, conv_w, cb3, wd_b, i, g_fin, bs, tm_s, last)
        fcs.append(st[0, :, SUBLANES - (CONV_W - 1):])

    diff_k_prompt = jnp.transpose(jnp.stack(dkp).reshape(n_even, bp, A_HEADS, 2, A_DH, sp), (0, 1, 5, 2, 3, 4))
    return (hp.reshape(bp, sp, d), hs.reshape(bs, ss, d),
            diff_k_prompt, jnp.stack(dvp), jnp.stack(plp), jnp.stack(skp), jnp.stack(svp), jnp.stack(fcp),
            jnp.stack(dks), jnp.stack(dvs), jnp.stack(pls), jnp.stack(sks), jnp.stack(svs), jnp.stack(fcs))
```

```python
import functools
import math

import jax
import jax.numpy as jnp
from jax import lax
from jax.experimental import pallas as pl
from jax.experimental.pallas import tpu as pltpu

F32 = jnp.float32
BF16 = jnp.bfloat16

D_MODEL = 1024
CHUNK = 64
ROPE_THETA = 10000.0
EPS = 1e-5
A_HEADS = 4
A_DH = 64
A_WIDTH = A_HEADS * 2 * A_DH
POOL_WINDOWS = (2, 4, 8, 16)
POOL_CH = 128
POOL_HIST = 15
B_WIDTH = 512
C_HEADS = 16
C_KV = 2
C_DH = 64
C_CACHE = 128
D_FF = 2816
CONV_W = 3

LOG2E = math.log2(math.e)
LANES = 128
SUBLANES = 8
HIST_ROWS = 16
FF_CHUNK = 256
N_FF_CHUNKS = D_FF // FF_CHUNK
ROW_TILE = 512
ATTN_TILE = 512
ATTN_QUERY_TILES = 4
SWA_TILE = 2 * CHUNK
VMEM_LIMIT = 56 * 1024 * 1024

NT_DIMS = (((1,), (1,)), ((), ()))


def _params(n_axes, vmem=VMEM_LIMIT):
    return pltpu.CompilerParams(dimension_semantics=("arbitrary",) * n_axes, vmem_limit_bytes=vmem)


def _rmsnorm_rows(x, g):
    return x * lax.rsqrt(jnp.mean(x * x, axis=-1, keepdims=True) + EPS) * g


def _rope128(z, cos, s_up, s_dn):
    return z * cos + pltpu.roll(z, 96, 1) * s_up + pltpu.roll(z, 32, 1) * s_dn


def _rope_tables(pos):
    inv = ROPE_THETA ** (-jnp.arange(0, A_DH, 2, dtype=F32) / A_DH)
    ang = pos.astype(F32)[:, None] * inv[None, :]
    cos, sin = jnp.cos(ang), jnp.sin(ang)
    zero = jnp.zeros_like(sin)
    cos128 = jnp.tile(cos, (1, 4))
    s_up = jnp.tile(jnp.concatenate([-sin, zero], axis=1), (1, 2))
    s_dn = jnp.tile(jnp.concatenate([zero, sin], axis=1), (1, 2))
    return cos128, s_up, s_dn


def _even_in_kernel(x_ref, g_ref, w_ref, cos_ref, sup_ref, sdn_ref,
                    q_ref, k_ref, kb_ref, v_ref, vx_ref, u_ref, *, prompt):
    tm = x_ref.shape[0]
    hn = _rmsnorm_rows(x_ref[...], g_ref[...]).astype(BF16)
    z = jnp.dot(hn, w_ref[...], preferred_element_type=F32)
    cos, s_up, s_dn = cos_ref[...], sup_ref[...], sdn_ref[...]
    for c in range(A_HEADS):
        sl = slice(c * LANES, (c + 1) * LANES)
        rq = _rope128(z[:, sl], cos, s_up, s_dn)
        q_ref[:, sl] = (rq * (A_DH ** -0.5 * LOG2E)).astype(BF16)
        rk = _rope128(z[:, A_WIDTH + c * LANES:A_WIDTH + (c + 1) * LANES], cos, s_up, s_dn)
        kb_ref[:, sl] = rk.astype(BF16)
        if prompt:
            k_ref[0, sl, :] = rk.T
        else:
            k_ref[:, sl] = rk
    v = z[:, 2 * A_WIDTH:3 * A_WIDTH]
    if prompt:
        for c in range(A_HEADS):
            v_ref[pl.ds(c, tm, stride=A_HEADS), :] = v[:, c * LANES:(c + 1) * LANES]
        vx_ref[0] = v.T.astype(BF16)
    else:
        v_ref[...] = v
        vx_ref[...] = v.astype(BF16)
    u_ref[...] = z[:, 3 * A_WIDTH:]


def _even_in(x, g, w, tabs, tm, nseq, prompt):
    t = x.shape[0]
    n_tab = tabs[0].shape[0] // tm
    nt = t // nseq // tm
    row = lambda i: (i, 0)
    tab = lambda i: (i % n_tab, 0)
    const = lambda i: (0, 0)
    if prompt:
        k_spec = pl.BlockSpec((1, A_WIDTH, tm), lambda i: (i // nt, 0, i % nt))
        k_shape = jax.ShapeDtypeStruct((nseq, A_WIDTH, t // nseq), F32)
        v_spec = pl.BlockSpec((tm * A_HEADS, LANES), row)
        v_shape = jax.ShapeDtypeStruct((t * A_HEADS, LANES), F32)
        vx_spec = pl.BlockSpec((1, A_WIDTH, tm), lambda i: (i, 0, 0))
        vx_shape = jax.ShapeDtypeStruct((t // tm, A_WIDTH, tm), BF16)
    else:
        k_spec = pl.BlockSpec((tm, A_WIDTH), row)
        k_shape = jax.ShapeDtypeStruct((t, A_WIDTH), F32)
        v_spec = pl.BlockSpec((tm, A_WIDTH), row)
        v_shape = jax.ShapeDtypeStruct((t, A_WIDTH), F32)
        vx_spec = pl.BlockSpec((tm, A_WIDTH), row)
        vx_shape = jax.ShapeDtypeStruct((t, A_WIDTH), BF16)
    return pl.pallas_call(
        functools.partial(_even_in_kernel, prompt=prompt),
        grid=(t // tm,),
        in_specs=[pl.BlockSpec((tm, D_MODEL), row), pl.BlockSpec((1, D_MODEL), const),
                  pl.BlockSpec(w.shape, const),
                  pl.BlockSpec((tm, LANES), tab), pl.BlockSpec((tm, LANES), tab), pl.BlockSpec((tm, LANES), tab)],
        out_specs=[pl.BlockSpec((tm, A_WIDTH), row), k_spec, pl.BlockSpec((tm, A_WIDTH), row),
                   v_spec, vx_spec,
                   pl.BlockSpec((tm, B_WIDTH), row)],
        out_shape=[jax.ShapeDtypeStruct((t, A_WIDTH), BF16), k_shape, jax.ShapeDtypeStruct((t, A_WIDTH), BF16),
                   v_shape, vx_shape,
                   jax.ShapeDtypeStruct((t, B_WIDTH), F32)],
        compiler_params=_params(1), name="even_in",
    )(x, g, w, *tabs)


def _odd_in_kernel(x_ref, g_ref, w_ref, b_ref, cos_ref, sup_ref, sdn_ref,
                   q_ref, k_ref, v_ref, kk_ref, vv_ref, *, transpose_v):
    tm = x_ref.shape[0]
    hn = _rmsnorm_rows(x_ref[...], g_ref[...]).astype(BF16)
    z = jnp.dot(hn, w_ref[...], preferred_element_type=F32) + b_ref[...]
    cos, s_up, s_dn = cos_ref[...], sup_ref[...], sdn_ref[...]
    nq = C_HEADS * C_DH
    for c in range(nq // LANES):
        sl = slice(c * LANES, (c + 1) * LANES)
        q_ref[:, sl] = (_rope128(z[:, sl], cos, s_up, s_dn) * (C_DH ** -0.5 * LOG2E)).astype(BF16)
    k = _rope128(z[:, nq:nq + LANES], cos, s_up, s_dn)
    v = z[:, nq + LANES:nq + 2 * LANES]
    tail = k_ref.shape[0]
    k_ref[...] = k[tm - tail:]
    v_ref[...] = v[tm - tail:]
    lo = lax.broadcasted_iota(jnp.int32, (tm, LANES), 1) < C_DH
    for src, dst in ((k, kk_ref),) if transpose_v else ((k, kk_ref), (v, vv_ref)):
        h0 = jnp.where(lo, src, 0.0)
        h1 = jnp.where(lo, 0.0, src)
        for n, val in enumerate((h0, pltpu.roll(h0, C_DH, 1), pltpu.roll(h1, C_DH, 1), h1)):
            dst[n] = val.astype(BF16)
    if transpose_v:
        vt = v.T.astype(BF16)
        zero = jnp.zeros((C_DH, tm), BF16)
        vv_ref[0] = jnp.concatenate([vt[:C_DH], zero], axis=0)
        vv_ref[1] = jnp.concatenate([zero, vt[:C_DH]], axis=0)
        vv_ref[2] = jnp.concatenate([vt[C_DH:], zero], axis=0)
        vv_ref[3] = jnp.concatenate([zero, vt[C_DH:]], axis=0)


def _odd_in(x, g, w, b, tabs, tm, nseq, tail, transpose_v):
    t = x.shape[0]
    n_tab = tabs[0].shape[0] // tm
    nt = t // nseq // tm
    nq = C_HEADS * C_DH
    row = lambda i: (i, 0)
    kv_spec = pl.BlockSpec((tail, LANES), lambda i: (i // nt, 0))
    kv_shape = jax.ShapeDtypeStruct((nseq * tail, LANES), F32)
    tab = lambda i: (i % n_tab, 0)
    const = lambda i: (0, 0)
    if transpose_v:
        vv_spec = pl.BlockSpec((4, LANES, tm), lambda i: (0, 0, i))
        vv_shape = jax.ShapeDtypeStruct((4, LANES, t), BF16)
    else:
        vv_spec = pl.BlockSpec((4, tm, LANES), lambda i: (0, i, 0))
        vv_shape = jax.ShapeDtypeStruct((4, t, LANES), BF16)
    return pl.pallas_call(
        functools.partial(_odd_in_kernel, transpose_v=transpose_v),
        grid=(t // tm,),
        in_specs=[pl.BlockSpec((tm, D_MODEL), row), pl.BlockSpec((1, D_MODEL), const),
                  pl.BlockSpec(w.shape, const), pl.BlockSpec((1, w.shape[1]), const),
                  pl.BlockSpec((tm, LANES), tab), pl.BlockSpec((tm, LANES), tab), pl.BlockSpec((tm, LANES), tab)],
        out_specs=[pl.BlockSpec((tm, nq), row), kv_spec, kv_spec,
                   pl.BlockSpec((4, tm, LANES), lambda i: (0, i, 0)), vv_spec],
        out_shape=[jax.ShapeDtypeStruct((t, nq), BF16), kv_shape, kv_shape,
                   jax.ShapeDtypeStruct((4, t, LANES), BF16), vv_shape],
        compiler_params=_params(1), name="odd_in",
    )(x, g, w, b, *tabs)


def _diff_lambda(lam_ref, lam_init):
    lp = lam_ref[...]
    return (jnp.exp(jnp.sum(lp[0:1] * lp[1:2], axis=-1, keepdims=True))
            - jnp.exp(jnp.sum(lp[2:3] * lp[3:4], axis=-1, keepdims=True)) + lam_init)


def _diff_prompt_kernel(q_ref, kb_ref, vt_ref, lam_ref, sub_ref, o_ref, m_ref, l_ref, acc_ref, sa_ref, sb_ref,
                        *, lam_init):
    i = pl.program_id(2)
    tq = q_ref.shape[0]
    tk = vt_ref.shape[2]
    q = q_ref[...].astype(F32)
    lo = lax.broadcasted_iota(jnp.int32, (tq, LANES), 1) < A_DH
    qm = (jnp.where(lo, q, 0.0).astype(BF16), jnp.where(lo, 0.0, q).astype(BF16))
    m_ref[...] = jnp.full(m_ref.shape, -jnp.inf, F32)
    l_ref[...] = jnp.zeros(l_ref.shape, F32)
    acc_ref[...] = jnp.zeros(acc_ref.shape, F32)

    def scores(j, s_ref, cols):
        start = pl.multiple_of(j * tk, tk)
        k = kb_ref[pl.ds(start, tk), :]
        for mp in range(2):
            s_ref[mp, :, cols] = lax.dot_general(k, qm[mp][cols], NT_DIMS, preferred_element_type=F32)

    def consume(j, s_ref, mask, cols):
        vt = vt_ref[j]
        for mp in range(2):
            s = s_ref[mp, :, cols]
            if mask is not None:
                s = jnp.where(mask, s, -jnp.inf)
            m_prev = m_ref[mp, :, cols]
            m_new = jnp.maximum(m_prev, jnp.max(s, axis=0, keepdims=True))
            alpha = jnp.exp2(m_prev - m_new)
            p = jnp.exp2(s - m_new)
            l_ref[mp, :, cols] = alpha * l_ref[mp, :, cols] + jnp.sum(p, axis=0, keepdims=True)
            acc_ref[mp, :, cols] = (alpha * acc_ref[mp, :, cols]
                                    + jnp.dot(vt, p.astype(BF16), preferred_element_type=F32))
            m_ref[mp, :, cols] = m_new

    every = slice(0, tq)
    reps = tq // tk
    assert reps % 4 == 0
    scores(0, sa_ref, every)

    def pair(j0):
        scores(j0 + 1, sb_ref, every)
        consume(j0, sa_ref, None, every)
        scores(j0 + 2, sa_ref, every)
        consume(j0 + 1, sb_ref, None, every)

    def quad(t, carry):
        pair(4 * t)
        pair(4 * t + 2)
        return carry

    lax.fori_loop(0, i * (reps // 4), quad, 0)
    k_chunk = lax.broadcasted_iota(jnp.int32, (tk, tq), 0) // CHUNK
    q_chunk = lax.broadcasted_iota(jnp.int32, (tk, tq), 1) // CHUNK
    causal = k_chunk <= q_chunk
    bufs = (sa_ref, sb_ref)
    for r in range(reps):
        if r + 1 < reps:
            scores(reps * i + r + 1, bufs[(r + 1) % 2], slice((r + 1) * tk, tq))
        consume(reps * i + r, bufs[r % 2], causal[:, :tq - r * tk], slice(r * tk, tq))

    lam = _diff_lambda(lam_ref, lam_init)
    a = acc_ref[0] / l_ref[0] - lam * (acc_ref[1] / l_ref[1])
    a = a * lax.rsqrt(jnp.mean(a * a, axis=0, keepdims=True) + EPS) * sub_ref[...] * (1.0 - lam_init)
    o_ref[...] = a.T.astype(BF16)


def _diff_prompt(q, kb, vt, lam_p, sub_col, batch, seq, lam_init):
    t = q.shape[0]
    tk = ATTN_TILE
    tq = ATTN_QUERY_TILES * tk
    nq = seq // tq
    assert vt.shape == (t // tk, A_WIDTH, tk)
    return pl.pallas_call(
        functools.partial(_diff_prompt_kernel, lam_init=lam_init),
        grid=(batch, A_HEADS, nq),
        in_specs=[pl.BlockSpec((tq, LANES), lambda b, h, i: (b * nq + i, h)),
                  pl.BlockSpec((seq, LANES), lambda b, h, i: (b, h)),
                  pl.BlockSpec((seq // tk, LANES, tk), lambda b, h, i: (b, h, 0)),
                  pl.BlockSpec((4, A_DH), lambda b, h, i: (0, 0)),
                  pl.BlockSpec((LANES, 1), lambda b, h, i: (0, 0))],
        out_specs=pl.BlockSpec((tq, LANES), lambda b, h, i: (b * nq + i, h)),
        out_shape=jax.ShapeDtypeStruct((t, A_WIDTH), BF16),
        scratch_shapes=[pltpu.VMEM((2, 1, tq), F32), pltpu.VMEM((2, 1, tq), F32),
                        pltpu.VMEM((2, LANES, tq), F32),
                        pltpu.VMEM((2, tk, tq), F32), pltpu.VMEM((2, tk, tq), F32)],
        compiler_params=_params(3), name="diff_attn_prompt",
    )(q, kb, vt, lam_p, sub_col)


def _diff_sample_kernel(q_ref, kn_ref, vn_ref, ckt_ref, cv_ref, lam_ref, sub_ref, o_ref, *, lam_init):
    sq = q_ref.shape[0]
    past = ckt_ref.shape[3]
    lo = lax.broadcasted_iota(jnp.int32, (sq, LANES), 1) < A_DH
    lam = _diff_lambda(lam_ref, lam_init)
    for h in range(A_HEADS):
        sl = slice(h * LANES, (h + 1) * LANES)
        q = q_ref[:, sl].astype(F32)
        q2 = jnp.concatenate([jnp.where(lo, q, 0.0), jnp.where(lo, 0.0, q)], axis=0).astype(BF16)
        s_c = jnp.dot(q2, ckt_ref[0, 0, sl, :].astype(BF16), preferred_element_type=F32)
        s_n = lax.dot_general(q2, kn_ref[:, sl], NT_DIMS, preferred_element_type=F32)
        m = jnp.maximum(jnp.max(s_c, axis=-1, keepdims=True), jnp.max(s_n, axis=-1, keepdims=True))
        e_c = jnp.exp2(s_c - m)
        e_n = jnp.exp2(s_n - m)
        den = jnp.sum(e_c, axis=-1, keepdims=True) + jnp.sum(e_n, axis=-1, keepdims=True)
        cv = cv_ref[0, 0, pl.ds(h, past, stride=A_HEADS), :].astype(BF16)
        o = (jnp.dot(e_c.astype(BF16), cv, preferred_element_type=F32)
             + jnp.dot(e_n.astype(BF16), vn_ref[:, sl], preferred_element_type=F32)) / den
        a = o[:sq] - lam * o[sq:]
        o_ref[:, sl] = (_rmsnorm_rows(a, sub_ref[...]) * (1.0 - lam_init)).astype(BF16)


def _diff_sample(q, kn, vn, cache_kt, cache_v, layer, lam_p, sub, lam_init):
    _, nb, _, past = cache_kt.shape
    t = q.shape[0]
    sq = t // nb
    row = lambda b: (b, 0)
    return pl.pallas_call(
        functools.partial(_diff_sample_kernel, lam_init=lam_init),
        grid=(nb,),
        in_specs=[pl.BlockSpec((sq, A_WIDTH), row), pl.BlockSpec((sq, A_WIDTH), row), pl.BlockSpec((sq, A_WIDTH), row),
                  pl.BlockSpec((1, 1, A_WIDTH, past), lambda b: (layer, b, 0, 0)),
                  pl.BlockSpec((1, 1, past * A_HEADS, LANES), lambda b: (layer, b, 0, 0)),
                  pl.BlockSpec((4, A_DH), lambda b: (0, 0)),
                  pl.BlockSpec((1, LANES), lambda b: (0, 0))],
        out_specs=pl.BlockSpec((sq, A_WIDTH), row),
        out_shape=jax.ShapeDtypeStruct((t, A_WIDTH), BF16),
        compiler_params=_params(1), name="diff_attn_sample",
    )(q, kn, vn, cache_kt, cache_v, lam_p, sub)


def _pool_kernel(a_ref, u_ref, hist_ref, pw_ref, ps_ref, o_ref, ext_ref, *, n_hist):
    i = pl.program_id(1)
    tm = u_ref.shape[0]

    @pl.when(i == 0)
    def _():
        ext_ref[0:HIST_ROWS, :] = hist_ref[0]

    u = u_ref[...]
    ext_ref[HIST_ROWS:HIST_ROWS + tm, :] = u
    pos = (i * tm + lax.broadcasted_iota(jnp.int32, (tm, 1), 0)).astype(F32)
    o_ref[:, 0:A_WIDTH] = a_ref[...]
    for g, win in enumerate(POOL_WINDOWS):
        sl = slice(g * POOL_CH, (g + 1) * POOL_CH)
        run = ext_ref[:, sl]
        span = 1
        while span < win:
            run = run + pltpu.roll(run, span, 0)
            span *= 2
        tot = run[HIST_ROWS:]
        cnt = jnp.minimum(float(win), pos + (1.0 + n_hist))
        pooled = (tot / cnt - u[:, sl]).astype(BF16)
        y = jnp.dot(pooled, pw_ref[g], preferred_element_type=F32) * ps_ref[:, sl]
        o_ref[:, A_WIDTH + g * POOL_CH:A_WIDTH + (g + 1) * POOL_CH] = y.astype(BF16)
    ext_ref[0:HIST_ROWS, :] = ext_ref[tm:tm + HIST_ROWS, :]


def _pool(a, u, hist, pool_w, pool_scale, nseq, tm, n_hist):
    t = a.shape[0]
    nt = t // nseq // tm
    row = lambda b, i: (b * nt + i, 0)
    return pl.pallas_call(
        functools.partial(_pool_kernel, n_hist=n_hist),
        grid=(nseq, nt),
        in_specs=[pl.BlockSpec((tm, A_WIDTH), row), pl.BlockSpec((tm, B_WIDTH), row),
                  pl.BlockSpec((1, HIST_ROWS, B_WIDTH), lambda b, i: (b, 0, 0)),
                  pl.BlockSpec(pool_w.shape, lambda b, i: (0, 0, 0)),
                  pl.BlockSpec((1, B_WIDTH), lambda b, i: (0, 0))],
        out_specs=pl.BlockSpec((tm, D_MODEL), row),
        out_shape=jax.ShapeDtypeStruct((t, D_MODEL), BF16),
        scratch_shapes=[pltpu.VMEM((HIST_ROWS + tm, B_WIDTH), F32)],
        compiler_params=_params(2), name="pool_mix",
    )(a, u, hist, pool_w, pool_scale)


def _swa_bias(tq):
    nk, nq = 2 * tq, 4 * tq
    kc = (jnp.arange(2 * nk) % nk // CHUNK)[:, None]
    qc = (jnp.arange(nq) % tq // CHUNK)[None, :]
    band = (kc >= qc) & (kc <= qc + 2)
    return jnp.where(jnp.stack([band & (kc >= 2), band]), 0.0, -jnp.inf).astype(F32)


def _swa_prompt_kernel(sink_ref, q_ref, kp_ref, kc_ref, vp_ref, vc_ref, bias_ref, o_ref, sa_ref, sb_ref):
    tq = q_ref.shape[0]
    nk = 2 * tq
    nq = 4 * tq
    pair = lax.broadcasted_iota(jnp.int32, (1, nq), 1) // tq
    low_rows = lax.broadcasted_iota(jnp.int32, (LANES, 1), 0) < C_DH

    def scores(kv, s_ref):
        qs = jnp.concatenate([q_ref[:, (kv * 4 + p) * LANES:(kv * 4 + p + 1) * LANES] for p in range(4)], axis=0)
        ks = jnp.concatenate([kp_ref[2 * kv], kc_ref[2 * kv], kp_ref[2 * kv + 1], kc_ref[2 * kv + 1]], axis=0)
        s_ref[...] = lax.dot_general(ks, qs, NT_DIMS, preferred_element_type=F32)

    def attend(kv, s_ref):
        vt = jnp.concatenate([vp_ref[2 * kv], vc_ref[2 * kv], vp_ref[2 * kv + 1], vc_ref[2 * kv + 1]], axis=1)
        s = s_ref[...] + bias_ref[0]
        es, rdens = [], []
        for half in range(2):
            sink = jnp.zeros((1, nq), F32)
            for p in range(4):
                sink = jnp.where(pair == p, sink_ref[kv * 8 + 2 * p + half] * LOG2E, sink)
            sh = s[half * nk:(half + 1) * nk]
            m = jnp.maximum(jnp.max(sh, axis=0, keepdims=True), sink)
            e = jnp.exp2(sh - m)
            rdens.append(1.0 / (jnp.sum(e, axis=0, keepdims=True) + jnp.exp2(sink - m)))
            es.append(e.astype(BF16))
        o = jnp.dot(vt, jnp.concatenate(es, axis=0), preferred_element_type=F32)
        o = (o * jnp.where(low_rows, rdens[0], rdens[1])).T
        for pr in range(4):
            o_ref[:, (kv * 4 + pr) * LANES:(kv * 4 + pr + 1) * LANES] = o[pr * tq:(pr + 1) * tq].astype(BF16)

    scores(0, sa_ref)
    scores(1, sb_ref)
    attend(0, sa_ref)
    attend(1, sb_ref)


def _swa_prompt(sinks, q, kk, vvt, batch, seq):
    t = q.shape[0]
    tq = SWA_TILE
    nt = seq // tq
    kspec = lambda f: pl.BlockSpec((4, tq, LANES), f)
    vspec = lambda f: pl.BlockSpec((4, LANES, tq), f)
    return pl.pallas_call(
        _swa_prompt_kernel,
        grid=(batch, nt),
        in_specs=[pl.BlockSpec(memory_space=pltpu.SMEM),
                  pl.BlockSpec((tq, C_HEADS * C_DH), lambda b, i: (b * nt + i, 0)),
                  kspec(lambda b, i: (0, b * nt + jnp.maximum(i - 1, 0), 0)), kspec(lambda b, i: (0, b * nt + i, 0)),
                  vspec(lambda b, i: (0, 0, b * nt + jnp.maximum(i - 1, 0))), vspec(lambda b, i: (0, 0, b * nt + i)),
                  pl.BlockSpec((1, 4 * tq, 4 * tq), lambda b, i: (jnp.minimum(i, 1), 0, 0))],
        out_specs=pl.BlockSpec((tq, C_HEADS * C_DH), lambda b, i: (b * nt + i, 0)),
        out_shape=jax.ShapeDtypeStruct((t, C_HEADS * C_DH), BF16),
        scratch_shapes=[pltpu.VMEM((4 * tq, 4 * tq), F32), pltpu.VMEM((4 * tq, 4 * tq), F32)],
        compiler_params=_params(2), name="swa_prompt",
    )(sinks, q, kk, kk, vvt, vvt, _swa_bias(tq))


def _swa_sample_kernel(sink_ref, q_ref, kk_ref, vv_ref, ck_ref, cv_ref, o_ref):
    sq = q_ref.shape[0]
    nc = ck_ref.shape[1]
    lo = lax.broadcasted_iota(jnp.int32, (nc, LANES), 1) < C_DH

    def halves(c, kv):
        own = jnp.where(lo, c, 0.0) if kv == 0 else jnp.where(lo, 0.0, c)
        swapped = pltpu.roll(own, C_DH, 1)
        return ((own, swapped) if kv == 0 else (swapped, own))

    ck = ck_ref[0]
    cv = cv_ref[0]
    for kv in range(C_KV):
        qs = jnp.concatenate([q_ref[:, (kv * 4 + p) * LANES:(kv * 4 + p + 1) * LANES] for p in range(4)], axis=0)
        ckh = halves(ck, kv)
        cvh = halves(cv, kv)
        o = jnp.zeros((4 * sq, LANES), F32)
        for half in range(2):
            ks = jnp.concatenate([ckh[half].astype(BF16), kk_ref[2 * kv + half]], axis=0)
            vs = jnp.concatenate([cvh[half].astype(BF16), vv_ref[2 * kv + half]], axis=0)
            s = lax.dot_general(qs, ks, NT_DIMS, preferred_element_type=F32)
            sink = jnp.concatenate(
                [jnp.full((sq, 1), sink_ref[kv * 8 + 2 * p + half] * LOG2E, F32) for p in range(4)], axis=0)
            m = jnp.maximum(jnp.max(s, axis=-1, keepdims=True), sink)
            e = jnp.exp2(s - m)
            p = e / (jnp.sum(e, axis=-1, keepdims=True) + jnp.exp2(sink - m))
            o = o + jnp.dot(p.astype(BF16), vs, preferred_element_type=F32)
        for pr in range(4):
            o_ref[:, (kv * 4 + pr) * LANES:(kv * 4 + pr + 1) * LANES] = o[pr * sq:(pr + 1) * sq].astype(BF16)


def _swa_sample(sinks, q, kk, vv, cache_k, cache_v):
    nb, nc, _ = cache_k.shape
    t = q.shape[0]
    sq = t // nb
    return pl.pallas_call(
        _swa_sample_kernel,
        grid=(nb,),
        in_specs=[pl.BlockSpec(memory_space=pltpu.SMEM),
                  pl.BlockSpec((sq, C_HEADS * C_DH), lambda b: (b, 0)),
                  pl.BlockSpec((4, sq, LANES), lambda b: (0, b, 0)),
                  pl.BlockSpec((4, sq, LANES), lambda b: (0, b, 0)),
                  pl.BlockSpec((1, nc, LANES), lambda b: (b, 0, 0)),
                  pl.BlockSpec((1, nc, LANES), lambda b: (b, 0, 0))],
        out_specs=pl.BlockSpec((sq, C_HEADS * C_DH), lambda b: (b, 0)),
        out_shape=jax.ShapeDtypeStruct((t, C_HEADS * C_DH), BF16),
        compiler_params=_params(1), name="swa_sample",
    )(sinks, q, kk, vv, cache_k, cache_v)


def _shift_rows(up, prev, shift, nseq):
    seq_len = up.shape[0] // nseq
    rolled = pltpu.roll(up, shift, 0)
    first = lax.broadcasted_iota(jnp.int32, (SUBLANES, up.shape[1]), 0) < shift
    pieces = []
    for s in range(nseq):
        lo = s * seq_len
        head = jnp.where(first, pltpu.roll(prev[s], shift, 0), pltpu.roll(up[lo:lo + SUBLANES], shift, 0))
        pieces += [head, rolled[lo + SUBLANES:lo + seq_len]]
    return jnp.concatenate(pieces, axis=0)


def _ffn_kernel(x_ref, mix_ref, wo_ref, g_ref, hist_ref, wu_ref, cw_ref, cb_ref, wd_ref, gf_ref,
                y_ref, st_ref, hn_ref, h_ref, carry_ref, *, nseq, final_norm):
    i = pl.program_id(1)
    tm = x_ref.shape[0]
    seq_len = tm // nseq

    @pl.when(i == 0)
    def _():
        carry_ref[...] = hist_ref[0]

    x = x_ref[...] + jnp.dot(mix_ref[...], wo_ref[0], preferred_element_type=F32)
    hn_ref[...] = _rmsnorm_rows(x, g_ref[...]).astype(BF16)
    for c in range(N_FF_CHUNKS):
        conv = []
        for sl in (slice(c * FF_CHUNK, (c + 1) * FF_CHUNK), slice(D_FF + c * FF_CHUNK, D_FF + (c + 1) * FF_CHUNK)):
            up = jnp.dot(hn_ref[...], wu_ref[0, :, sl], preferred_element_type=F32)
            prev = [carry_ref[s, :, sl] for s in range(nseq)]
            conv.append(cb_ref[0, :, sl] + _shift_rows(up, prev, 2, nseq) * cw_ref[0, 0:1, sl]
                        + _shift_rows(up, prev, 1, nseq) * cw_ref[0, 1:2, sl] + up * cw_ref[0, 2:3, sl])
            for s in range(nseq):
                carry_ref[s, :, sl] = up[(s + 1) * seq_len - SUBLANES:(s + 1) * seq_len]
        gate, val = conv
        h_ref[:, c * FF_CHUNK:(c + 1) * FF_CHUNK] = (gate * jax.nn.sigmoid(gate) * val).astype(BF16)
    y = x + jnp.dot(h_ref[...], wd_ref[0], preferred_element_type=F32)
    if final_norm:
        y = _rmsnorm_rows(y, gf_ref[...])
    y_ref[...] = y
    st_ref[0] = carry_ref[...]


def _ffn(x, mix, wo, mix_layer, g, hist, wu, cw, cb, wd, layer, g_final, nseq, tm, final_norm):
    t = x.shape[0]
    n_outer = hist.shape[0]
    nt = t // n_outer // tm
    row = lambda b, i: (b * nt + i, 0)
    c2 = lambda b, i: (0, 0)
    lay = lambda b, i: (layer, 0, 0)
    once = pl.Buffered(1)
    st_block = (1,) + hist.shape[1:]
    st_map = lambda b, i: (b, 0, 0, 0)
    return pl.pallas_call(
        functools.partial(_ffn_kernel, nseq=nseq, final_norm=final_norm),
        grid=(n_outer, nt),
        in_specs=[pl.BlockSpec((tm, D_MODEL), row), pl.BlockSpec((tm, D_MODEL), row),
                  pl.BlockSpec((1,) + wo.shape[1:], lambda b, i: (mix_layer, 0, 0), pipeline_mode=once),
                  pl.BlockSpec((1, D_MODEL), c2),
                  pl.BlockSpec(st_block, st_map),
                  pl.BlockSpec((1,) + wu.shape[1:], lay, pipeline_mode=once), pl.BlockSpec((1,) + cw.shape[1:], lay),
                  pl.BlockSpec((1,) + cb.shape[1:], lay), pl.BlockSpec((1,) + wd.shape[1:], lay, pipeline_mode=once),
                  pl.BlockSpec((1, D_MODEL), c2)],
        out_specs=[pl.BlockSpec((tm, D_MODEL), row), pl.BlockSpec(st_block, st_map)],
        out_shape=[jax.ShapeDtypeStruct((t, D_MODEL), F32), jax.ShapeDtypeStruct(hist.shape, F32)],
        scratch_shapes=[pltpu.VMEM((tm, D_MODEL), BF16),
                        pltpu.VMEM((tm, D_FF), BF16),
                        pltpu.VMEM(hist.shape[1:], F32)],
        compiler_params=_params(2), name="conv_ffn",
    )(x, mix, wo, g, hist, wu, cw, cb, wd, g_final)


def kernel(x_prompt, x_sample, cache_diff_k, cache_diff_v, state_pool, cache_swa_k, cache_swa_v, state_ffn_conv,
           norm_attn, norm_ffn, norm_final, w_in_even, w_out_even, diff_lambda, diff_subln, pool_w, pool_scale,
           w_in_odd, b_in_odd, w_out_odd, sinks, w_up, conv_w, conv_b, w_down):
    bp, sp, d = x_prompt.shape
    bs, ss, _ = x_sample.shape
    depth = norm_attn.shape[0]
    past = cache_diff_k.shape[2]
    hp = x_prompt.reshape(bp * sp, d)
    hs = x_sample.reshape(bs * ss, d)
    tm_s = bs * ss

    tabs_p = _rope_tables(jnp.arange(sp, dtype=jnp.int32))
    tabs_s = tuple(jnp.tile(t, (bs, 1)) for t in _rope_tables(past + jnp.arange(ss, dtype=jnp.int32)))

    wu_b = w_up.astype(BF16)
    wo_even = w_out_even.astype(BF16)
    wo_odd = w_out_odd.astype(BF16)
    wd_b = w_down.astype(BF16)
    cb3 = conv_b[:, None, :]
    conv_pad = jnp.zeros((bs, SUBLANES - (CONV_W - 1), 2 * D_FF), F32)
    zero_conv = jnp.zeros((bp, 1, SUBLANES, 2 * D_FF), F32)
    zero_pool = jnp.zeros((bp, HIST_ROWS, B_WIDTH), F32)
    n_even = cache_diff_k.shape[0]
    cache_kt = jnp.transpose(cache_diff_k, (0, 1, 3, 4, 5, 2)).reshape(n_even, bs, A_WIDTH, past)
    cache_vr = cache_diff_v.reshape(n_even, bs, past * A_HEADS, 2 * A_DH)

    dkp, dvp, plp, skp, svp, fcp = [], [], [], [], [], []
    dks, dvs, pls, sks, svs, fcs = [], [], [], [], [], []
    for i in range(depth):
        j = i // 2
        g_attn = norm_attn[i][None, :]
        if i % 2 == 0:
            lam_init = 0.8 - 0.6 * math.exp(-0.3 * i)
            w_in = w_in_even[j].astype(BF16)
            pw = pool_w[j].astype(BF16)
            ps = pool_scale[j][None, :]
            lam_p = diff_lambda[j]
            wo = wo_even

            q, kt, kb, v, vt, u = _even_in(hp, g_attn, w_in, tabs_p, ROW_TILE, bp, True)
            a = _diff_prompt(q, kb, vt, lam_p, diff_subln[j][:, None], bp, sp, lam_init)
            mix_p = _pool(a, u, zero_pool, pw, ps, bp, 2 * ROW_TILE, 0)
            dkp.append(kt)
            dvp.append(v.reshape(bp, sp, A_HEADS, 2 * A_DH))
            plp.append(u.reshape(bp, sp, B_WIDTH)[:, sp - POOL_HIST:])

            q, k, kb, v, vb, u = _even_in(hs, g_attn, w_in, tabs_s, tm_s, 1, False)
            a = _diff_sample(q, kb, vb, cache_kt, cache_vr, j, lam_p, diff_subln[j][None, :], lam_init)
            hist = jnp.concatenate([jnp.zeros((bs, HIST_ROWS - POOL_HIST, B_WIDTH), F32), state_pool[j]], axis=1)
            mix_s = _pool(a, u, hist, pw, ps, bs, ss, POOL_HIST)
            dks.append(k.reshape(bs, ss, A_HEADS, 2, A_DH))
            dvs.append(v.reshape(bs, ss, A_HEADS, 2 * A_DH))
            pls.append(u.reshape(bs, ss, B_WIDTH)[:, ss - POOL_HIST:])
        else:
            w_in = w_in_odd[j].astype(BF16)
            b_in = b_in_odd[j][None, :]
            sk = sinks[j]
            wo = wo_odd

            q, k, v, kk, vvt = _odd_in(hp, g_attn, w_in, b_in, tabs_p, ROW_TILE, bp, C_CACHE, True)
            mix_p = _swa_prompt(sk, q, kk, vvt, bp, sp)
            skp.append(k.reshape(bp, C_CACHE, C_KV, C_DH))
            svp.append(v.reshape(bp, C_CACHE, C_KV, C_DH))

            q, k, v, kk, vv = _odd_in(hs, g_attn, w_in, b_in, tabs_s, tm_s, 1, tm_s, False)
            mix_s = _swa_sample(sk, q, kk, vv, cache_swa_k[j].reshape(bs, C_CACHE, C_KV * C_DH),
                                cache_swa_v[j].reshape(bs, C_CACHE, C_KV * C_DH))
            k_all = jnp.concatenate([cache_swa_k[j], k.reshape(bs, ss, C_KV, C_DH)], axis=1)
            v_all = jnp.concatenate([cache_swa_v[j], v.reshape(bs, ss, C_KV, C_DH)], axis=1)
            sks.append(k_all[:, -C_CACHE:])
            svs.append(v_all[:, -C_CACHE:])

        last = i == depth - 1
        g_ffn = norm_ffn[i][None, :]
        g_fin = norm_final[None, :]
        hp, st = _ffn(hp, mix_p, wo, j, g_ffn, zero_conv, wu_b, conv_w, cb3, wd_b, i, g_fin, 1, ROW_TILE, last)
        fcp.append(st[:, 0, SUBLANES - (CONV_W - 1):])
        hist = jnp.concatenate([conv_pad, state_ffn_conv[i]], axis=1)[None]
        hs, st = _ffn(hs, mix_s, wo, j, g_ffn, hist, wu_b, conv_w, cb3, wd_b, i, g_fin, bs, tm_s, last)
        fcs.append(st[0, :, SUBLANES - (CONV_W - 1):])

    diff_k_prompt = jnp.transpose(jnp.stack(dkp).reshape(n_even, bp, A_HEADS, 2, A_DH, sp), (0, 1, 5, 2, 3, 4))
    return (hp.reshape(bp, sp, d), hs.reshape(bs, ss, d),
            diff_k_prompt, jnp.stack(dvp), jnp.stack(plp), jnp.stack(skp), jnp.stack(svp), jnp.stack(fcp),
            jnp.stack(dks), jnp.stack(dvs), jnp.stack(pls), jnp.stack(sks), jnp.stack(svs), jnp.stack(fcs))
```
